```python
import jax, jax.numpy as jnp
from jax import lax
import numpy as np

D_MODEL = 1024
BATCH = 32
SEQ = 2048
DEPTH = 4

SB_HEADS = 8
SB_HEAD_DIM = 64
SB_WIDTH = SB_HEADS * SB_HEAD_DIM
Q_BLOCK = 128
ML_HEADS = 4
ML_HEAD_DIM = 128
ML_WIDTH = ML_HEADS * ML_HEAD_DIM
ML_CHUNK = 64
CONV_WIDTH = 4
N_BRANCH = 2
SPLIT_SIZES = (SB_WIDTH, SB_WIDTH, SB_WIDTH, SB_WIDTH,
               ML_WIDTH, ML_WIDTH, ML_WIDTH, ML_WIDTH, ML_WIDTH,
               ML_HEADS, ML_HEADS, D_MODEL, D_MODEL)
IN_COLS = 4 * SB_WIDTH + 5 * ML_WIDTH + 2 * ML_HEADS + N_BRANCH * D_MODEL
F_GATE_OFFSET = 4 * SB_WIDTH + 5 * ML_WIDTH + ML_HEADS
DEEPNORM_ALPHA = (2.0 * DEPTH) ** 0.25
DEEPNORM_BETA = (8.0 * DEPTH) ** -0.25
LN_EPS = 1e-5

kernel_name = "hybrid_stickbreak_mlstm_deepnorm_adaln"


def layer_norm(x, g, b):
    xf = x.astype(jnp.float32)
    mu = xf.mean(-1, keepdims=True)
    var = jnp.mean(jnp.square(xf - mu), -1, keepdims=True)
    return ((xf - mu) * lax.rsqrt(var + LN_EPS) * g + b).astype(x.dtype)


def causal_depthwise_conv(x, w, b):
    C = x.shape[-1]
    y = lax.conv_general_dilated(
        x, w[:, None, :].astype(x.dtype), window_strides=(1,),
        padding=[(CONV_WIDTH - 1, 0)], dimension_numbers=('NWC', 'WIO', 'NWC'),
        feature_group_count=C)
    return y + b


def to_heads(a, n_heads):
    B, S, W = a.shape
    return a.reshape(B, S, n_heads, W // n_heads).transpose(0, 2, 1, 3)


def from_heads(a):
    B, H, S, d = a.shape
    return a.transpose(0, 2, 1, 3).reshape(B, S, H * d)


def stick_breaking_attention(q, k, v):
    B, H, S, d = q.shape
    scale = d ** -0.5
    outs = []
    for blk in range(S // Q_BLOCK):
        q0 = blk * Q_BLOCK
        kend = q0 + Q_BLOCK
        qb = q[:, :, q0:kend]
        kb = k[:, :, :kend]
        vb = v[:, :, :kend]
        z = jnp.einsum('bhtd,bhsd->bhts', qb, kb).astype(jnp.float32) * scale
        t_idx = q0 + jnp.arange(Q_BLOCK)[:, None]
        s_idx = jnp.arange(kend)[None, :]
        causal = s_idx < t_idx
        log_beta = jax.nn.log_sigmoid(z)
        log_rem = jnp.where(causal, jax.nn.log_sigmoid(-z), 0.0)
        suffix = lax.cumsum(log_rem, axis=3, reverse=True) - log_rem
        attn = jnp.where(causal, jnp.exp(log_beta + suffix), 0.0)
        outs.append(jnp.einsum('bhts,bhsd->bhtd', attn.astype(vb.dtype), vb))
    return jnp.concatenate(outs, axis=2)


def mlstm_chunkwise(q, k, v, i_pre, log_f):
    B, H, S, dk = q.shape
    dv = v.shape[-1]
    L = ML_CHUNK
    NC = S // L
    q = q.astype(jnp.float32) * dk ** -0.5
    k = k.astype(jnp.float32)
    v = v.astype(jnp.float32)

    def to_chunks(a):
        return jnp.moveaxis(a.reshape(B, H, NC, L, *a.shape[3:]), 2, 0)

    xs = (to_chunks(q), to_chunks(k), to_chunks(v), to_chunks(i_pre), to_chunks(log_f))
    causal = jnp.tril(jnp.ones((L, L), dtype=bool))

    def step(carry, xs_c):
        C, n, m = carry
        qb, kb, vb, ib, fb = xs_c
        b = jnp.cumsum(fb, axis=-1)
        b_last = b[..., -1]
        D = jnp.where(causal, b[..., :, None] - b[..., None, :] + ib[..., None, :], -jnp.inf)
        inter = b + m[..., None]
        m_t = jnp.maximum(inter, D.max(-1))
        w_inter = jnp.exp(inter - m_t)
        s_mat = jnp.exp(D - m_t[..., None]) * jnp.einsum('bhtd,bhsd->bhts', qb, kb)
        num = (w_inter[..., None] * jnp.einsum('bhtd,bhde->bhte', qb, C)
               + jnp.einsum('bhts,bhse->bhte', s_mat, vb))
        den = w_inter * jnp.einsum('bhtd,bhd->bht', qb, n) + s_mat.sum(-1)
        h = num / jnp.maximum(jnp.abs(den), jnp.exp(-m_t))[..., None]
        g = b_last[..., None] - b + ib
        m_new = jnp.maximum(b_last + m, g.max(-1))
        decay = jnp.exp(b_last + m - m_new)
        wk = jnp.exp(g - m_new[..., None])
        C_new = decay[..., None, None] * C + jnp.einsum('bhs,bhsd,bhse->bhde', wk, kb, vb)
        n_new = decay[..., None] * n + jnp.einsum('bhs,bhsd->bhd', wk, kb)
        return (C_new, n_new, m_new), h

    init = (jnp.zeros((B, H, dk, dv), jnp.float32),
            jnp.zeros((B, H, dk), jnp.float32),
            jnp.zeros((B, H), jnp.float32))
    _, hs = lax.scan(step, init, xs)
    return jnp.moveaxis(hs, 0, 2).reshape(B, H, S, dv)


def head_norm(h, g):
    mu = h.mean(-1, keepdims=True)
    var = jnp.mean(jnp.square(h - mu), -1, keepdims=True)
    return from_heads((h - mu) * lax.rsqrt(var + LN_EPS)) * g


def hybrid_layer(x, c, w_ada, b_ada, w_in, b_in, conv_w, conv_b, ml_norm_g,
                 w_sb_proj, w_ml_proj, w_out, ln_g, ln_b):
    dtype = x.dtype
    mod = jax.nn.silu(c) @ w_ada + b_ada
    shift, scale, gate = jnp.split(mod, 3, axis=-1)
    h = x * (1.0 + scale[:, None, :]) + shift[:, None, :]

    proj = h @ w_in + b_in
    points = []
    acc = 0
    for size in SPLIT_SIZES[:-1]:
        acc += size
        points.append(acc)
    (sb_q, sb_k, sb_v, sb_z, ml_q, ml_k, ml_v, ml_o, ml_z,
     ml_i, ml_f, g_sb, g_ml) = jnp.split(proj, points, axis=-1)

    y_sb = stick_breaking_attention(to_heads(sb_q, SB_HEADS), to_heads(sb_k, SB_HEADS),
                                    to_heads(sb_v, SB_HEADS))
    y_sb = (from_heads(y_sb) * jax.nn.silu(sb_z)) @ w_sb_proj

    qk = jax.nn.silu(causal_depthwise_conv(jnp.concatenate([ml_q, ml_k], -1), conv_w, conv_b))
    mq, mk = jnp.split(qk, 2, axis=-1)
    i_pre = ml_i.astype(jnp.float32).transpose(0, 2, 1)
    log_f = jax.nn.log_sigmoid(ml_f.astype(jnp.float32)).transpose(0, 2, 1)
    hm = mlstm_chunkwise(to_heads(mq, ML_HEADS), to_heads(mk, ML_HEADS),
                         to_heads(ml_v, ML_HEADS), i_pre, log_f)
    hm = head_norm(hm, ml_norm_g).astype(dtype)
    y_ml = (hm * jax.nn.sigmoid(ml_o) * jax.nn.silu(ml_z)) @ w_ml_proj

    y = (jax.nn.sigmoid(g_sb) * y_sb + jax.nn.sigmoid(g_ml) * y_ml) @ w_out

    return layer_norm(DEEPNORM_ALPHA * x + (1.0 + gate[:, None, :]) * y, ln_g, ln_b)


def setup_inputs(seed: int = 0) -> dict:
    key = jax.random.key(seed)
    ks = jax.random.split(key, 14)
    D = D_MODEL
    nrm = lambda k, shape: jax.random.normal(k, shape, jnp.float32)
    x = nrm(ks[0], (BATCH, SEQ, D))
    c = nrm(ks[1], (BATCH, D))
    w_ada = nrm(ks[2], (DEPTH, D, 3 * D)) * (0.1 * D ** -0.5)
    b_ada = 0.01 * nrm(ks[3], (DEPTH, 3 * D))
    w_in = nrm(ks[4], (DEPTH, D, IN_COLS)) * D ** -0.5
    b_in = 0.02 * nrm(ks[5], (DEPTH, IN_COLS))
    b_in = b_in.at[:, F_GATE_OFFSET:F_GATE_OFFSET + ML_HEADS].add(
        jnp.linspace(3.0, 6.0, ML_HEADS, dtype=jnp.float32))
    conv_w = nrm(ks[6], (DEPTH, CONV_WIDTH, 2 * ML_WIDTH)) * CONV_WIDTH ** -0.5
    conv_b = 0.02 * nrm(ks[7], (DEPTH, 2 * ML_WIDTH))
    ml_norm_g = 1.0 + 0.02 * nrm(ks[8], (DEPTH, ML_WIDTH))
    w_sb_proj = nrm(ks[9], (DEPTH, SB_WIDTH, D)) * (SB_WIDTH ** -0.5 * DEEPNORM_BETA)
    w_ml_proj = nrm(ks[10], (DEPTH, ML_WIDTH, D)) * (ML_WIDTH ** -0.5 * DEEPNORM_BETA)
    w_out = nrm(ks[11], (DEPTH, D, D)) * (D ** -0.5 * DEEPNORM_BETA)
    ln_g = 1.0 + 0.02 * nrm(ks[12], (DEPTH, D))
    ln_b = 0.02 * nrm(ks[13], (DEPTH, D))
    return {"x": x, "c": c, "w_ada": w_ada, "b_ada": b_ada, "w_in": w_in, "b_in": b_in,
            "conv_w": conv_w, "conv_b": conv_b, "ml_norm_g": ml_norm_g,
            "w_sb_proj": w_sb_proj, "w_ml_proj": w_ml_proj, "w_out": w_out,
            "ln_g": ln_g, "ln_b": ln_b}


def reference(x, c, w_ada, b_ada, w_in, b_in, conv_w, conv_b, ml_norm_g,
              w_sb_proj, w_ml_proj, w_out, ln_g, ln_b):
    for l in range(DEPTH):
        x = hybrid_layer(x, c, w_ada[l], b_ada[l], w_in[l], b_in[l], conv_w[l], conv_b[l],
                         ml_norm_g[l], w_sb_proj[l], w_ml_proj[l], w_out[l], ln_g[l], ln_b[l])
    return x
```

```python
import functools

import jax
import jax.numpy as jnp
from jax import lax
from jax.experimental import pallas as pl
from jax.experimental.pallas import tpu as pltpu

F32 = jnp.float32
BF16 = jnp.bfloat16

SB_HEADS = 8
SB_HEAD_DIM = 64
SB_WIDTH = SB_HEADS * SB_HEAD_DIM
ML_HEADS = 4
ML_HEAD_DIM = 128
ML_WIDTH = ML_HEADS * ML_HEAD_DIM
CONV_WIDTH = 4
LN_EPS = 1e-5

LANES = 128
SUBLANES = 8
VMEM_LIMIT_BYTES = 56 * 1024 * 1024

COL_GROUP = 512
D_MODEL = 1024
SB0 = 2 * D_MODEL
ML_Q0 = SB0 + 4 * SB_WIDTH

TOKEN_TILE = 512
SB_TILE = 256
ML_CHUNK = 256


def _dot(a, b):
    return jnp.dot(a, b, preferred_element_type=F32)


def _dot_nt(a, b):
    return lax.dot_general(a, b, (((1,), (1,)), ((), ())), preferred_element_type=F32)


def _dot_tn(a, b):
    return lax.dot_general(a, b, (((0,), (0,)), ((), ())), preferred_element_type=F32)


def _split_bf16(x):
    hi = x.astype(BF16)
    lo = (x - hi.astype(F32)).astype(BF16)
    return hi, lo


def _sigmoid(x):
    return 1.0 / (1.0 + jnp.exp(-x))


def _silu(x):
    return x * _sigmoid(x)


def _adaln_kernel(c_ref, w_ref, b_ref, o_ref):
    c = c_ref[...]
    o_ref[0] = _dot(_silu(c).astype(BF16), w_ref[0].astype(BF16)) + b_ref[0]


def _adaln(c, w_ada, b_ada):
    depth, d, _ = w_ada.shape
    bsz = c.shape[0]
    return pl.pallas_call(
        _adaln_kernel,
        grid=(depth, 3),
        in_specs=[
            pl.BlockSpec((bsz, d), lambda l, j: (0, 0)),
            pl.BlockSpec((1, d, d), lambda l, j: (l, 0, j)),
            pl.BlockSpec((1, 1, d), lambda l, j: (l, 0, j)),
        ],
        out_specs=pl.BlockSpec((1, bsz, d), lambda l, j: (l, 0, j)),
        out_shape=jax.ShapeDtypeStruct((depth, bsz, 3 * d), F32),
        compiler_params=pltpu.CompilerParams(
            dimension_semantics=("arbitrary", "arbitrary"), vmem_limit_bytes=VMEM_LIMIT_BYTES),
        name="adaln",
    )(c, w_ada, b_ada.reshape(depth, 1, 3 * d))


def _inproj_kernel(x_ref, xh_ref, mod_ref, w_ref, b_ref, wif_ref, bif_ref, cw_ref, cb_ref,
                   main_ref, if_ref, *, tiles_per_seq):
    first = (pl.program_id(0) % tiles_per_seq) == 0
    shift = mod_ref[0, 0]
    scale = mod_ref[0, 1]
    h = (x_ref[...] * (1.0 + scale) + shift).astype(BF16)
    h_halo = (xh_ref[...] * (1.0 + scale) + shift).astype(BF16)

    if_ref[...] = _dot(h, wif_ref[...]) + bif_ref[...]

    n_cols = main_ref.shape[1]
    for c0 in range(0, n_cols, COL_GROUP):
        cs = slice(c0, c0 + COL_GROUP)
        acc = _dot(h, w_ref[:, cs]) + b_ref[:, cs]
        if SB0 <= c0 < SB0 + SB_WIDTH:
            acc = acc * (SB_HEAD_DIM ** -0.5)
        elif ML_Q0 <= c0 < ML_Q0 + 2 * ML_WIDTH:
            ks = slice(c0 - ML_Q0, c0 - ML_Q0 + COL_GROUP)
            halo = _dot(h_halo, w_ref[:, cs]) + b_ref[:, cs]
            halo = jnp.where(first, 0.0, halo)
            cat = jnp.concatenate([halo, acc], axis=0)
            y = cb_ref[:, ks] + cw_ref[CONV_WIDTH - 1:CONV_WIDTH, ks] * acc
            for j in range(CONV_WIDTH - 1):
                shifted = pltpu.roll(cat, CONV_WIDTH - 1 - j, axis=0)[SUBLANES:, :]
                y = y + cw_ref[j:j + 1, ks] * shifted
            acc = _silu(y)
            if c0 < ML_Q0 + ML_WIDTH:
                acc = acc * (ML_HEAD_DIM ** -0.5)
        main_ref[:, cs] = acc.astype(BF16)


def _inproj(x2, mod, wm, bm, wif, bif, cw, cb, *, seq):
    t, d = x2.shape
    tm = TOKEN_TILE
    n_cols = wm.shape[1]
    tiles_per_seq = seq // tm
    const = lambda i: (0, 0)
    return pl.pallas_call(
        functools.partial(_inproj_kernel, tiles_per_seq=tiles_per_seq),
        grid=(t // tm,),
        in_specs=[
            pl.BlockSpec((tm, d), lambda i: (i, 0)),
            pl.BlockSpec((SUBLANES, d), lambda i: (jnp.maximum(i * (tm // SUBLANES) - 1, 0), 0)),
            pl.BlockSpec((1, 3, 1, d), lambda i: (i // tiles_per_seq, 0, 0, 0)),
            pl.BlockSpec((d, n_cols), const),
            pl.BlockSpec((1, n_cols), const),
            pl.BlockSpec((d, LANES), const),
            pl.BlockSpec((1, LANES), const),
            pl.BlockSpec((CONV_WIDTH, 2 * ML_WIDTH), const),
            pl.BlockSpec((1, 2 * ML_WIDTH), const),
        ],
        out_specs=[
            pl.BlockSpec((tm, n_cols), lambda i: (i, 0)),
            pl.BlockSpec((tm, LANES), lambda i: (i, 0)),
        ],
        out_shape=[
            jax.ShapeDtypeStruct((t, n_cols), BF16),
            jax.ShapeDtypeStruct((t, LANES), F32),
        ],
        compiler_params=pltpu.CompilerParams(
            dimension_semantics=("arbitrary",), vmem_limit_bytes=VMEM_LIMIT_BYTES),
        name="inproj",
    )(x2, x2, mod, wm, bm, wif, bif, cw, cb)


def _sb_kernel(q_ref, k_ref, v_ref, z_ref, o_ref, *, seq):
    ts = SB_TILE
    row = lax.broadcasted_iota(jnp.int32, (ts, ts), 0)
    col = lax.broadcasted_iota(jnp.int32, (ts, ts), 1)
    strict = col < row
    upper = jnp.where(row > col, 1.0, 0.0).astype(BF16)
    head0 = lax.broadcasted_iota(jnp.int32, (ts, LANES), 1) < SB_HEAD_DIM

    def tile(qm, k, v, carry, acc, diag):
        z = _dot_nt(qm, k)
        sp = jnp.maximum(z, 0.0) + jnp.log(1.0 + jnp.exp(-jnp.abs(z)))
        log_rem = -sp
        if diag:
            log_rem = jnp.where(strict, log_rem, 0.0)
        hi, lo = _split_bf16(log_rem)
        suffix = _dot(hi, upper) + _dot(lo, upper)
        attn = jnp.exp((z - sp) + suffix + carry)
        if diag:
            attn = jnp.where(strict, attn, 0.0)
        acc = acc + _dot(attn.astype(BF16), v)
        carry = carry + jnp.sum(log_rem, axis=-1, keepdims=True)
        return carry, acc

    def q_body(i, _):
        q0 = pl.multiple_of(i * ts, ts)
        q = q_ref[pl.ds(q0, ts), :]
        zero = jnp.zeros_like(q)
        qs = (jnp.where(head0, q, zero), jnp.where(head0, zero, q))
        k = k_ref[pl.ds(q0, ts), :]
        v = v_ref[pl.ds(q0, ts), :]
        init = (jnp.zeros((ts, 1), F32), jnp.zeros((ts, LANES), F32))
        state = tuple(tile(qs[hd], k, v, *init, True) for hd in range(2))

        def k_body(jj, st):
            j0 = pl.multiple_of((i - 1 - jj) * ts, ts)
            kj = k_ref[pl.ds(j0, ts), :]
            vj = v_ref[pl.ds(j0, ts), :]
            return tuple(tile(qs[hd], kj, vj, *st[hd], False) for hd in range(2))

        state = lax.fori_loop(0, i, k_body, state)
        acc = jnp.where(head0, state[0][1], state[1][1])
        zg = z_ref[pl.ds(q0, ts), :].astype(F32)
        o_ref[pl.ds(q0, ts), :] = (acc * _silu(zg)).astype(BF16)
        return 0

    lax.fori_loop(0, seq // ts, q_body, 0)


def _sb_attention(main, *, batch, seq):
    t = main.shape[0]
    n_pairs = SB_WIDTH // LANES
    first = SB0 // LANES
    blk = lambda g: pl.BlockSpec((seq, LANES), lambda b, p: (b, first + g * n_pairs + p))
    return pl.pallas_call(
        functools.partial(_sb_kernel, seq=seq),
        grid=(batch, n_pairs),
        in_specs=[blk(0), blk(1), blk(2), blk(3)],
        out_specs=pl.BlockSpec((seq, LANES), lambda b, p: (b, p)),
        out_shape=jax.ShapeDtypeStruct((t, SB_WIDTH), BF16),
        compiler_params=pltpu.CompilerParams(
            dimension_semantics=("arbitrary", "arbitrary"), vmem_limit_bytes=VMEM_LIMIT_BYTES),
        name="sb_attention",
    )(main, main, main, main)


def _mlstm_kernel(q_ref, k_ref, v_ref, o_ref, z_ref, if_ref, g_ref, out_ref, c_scr, m_scr):
    cl = q_ref.shape[0]
    hd = ML_HEAD_DIM

    @pl.when(pl.program_id(1) == 0)
    def _():
        c_scr[...] = jnp.zeros_like(c_scr)
        m_scr[...] = jnp.zeros_like(m_scr)

    row = lax.broadcasted_iota(jnp.int32, (cl, cl), 0)
    col = lax.broadcasted_iota(jnp.int32, (cl, cl), 1)
    tril = col <= row
    tril_bf = jnp.where(tril, 1.0, 0.0).astype(BF16)

    gates = if_ref[...]
    log_f = jnp.minimum(gates, 0.0) - jnp.log(1.0 + jnp.exp(-jnp.abs(gates)))
    f_hi, f_lo = _split_bf16(log_f)
    b_all = _dot(tril_bf, f_hi) + _dot(tril_bf, f_lo)
    lane = lax.broadcasted_iota(jnp.int32, (cl, LANES), 1)
    rows_t = jnp.where(lane < ML_HEADS, gates, b_all).T
    ones = jnp.ones((cl, hd), BF16)

    for h in range(ML_HEADS):
        hs = slice(h * hd, (h + 1) * hd)
        q = q_ref[:, hs]
        k = k_ref[:, hs]
        v_aug = jnp.concatenate([v_ref[:, hs], ones], axis=1)
        b_col = b_all[:, ML_HEADS + h:ML_HEADS + h + 1]
        i_col = gates[:, h:h + 1]
        b_row = rows_t[ML_HEADS + h:ML_HEADS + h + 1, :]
        i_row = rows_t[h:h + 1, :]
        m_prev = m_scr[h][:, 0:1]
        c_prev = c_scr[h]

        d_mat = jnp.where(tril, b_col + (i_row - b_row), -jnp.inf)
        inter = b_col + m_prev
        m_t = jnp.maximum(inter, jnp.max(d_mat, axis=-1, keepdims=True))
        w_inter = jnp.exp(inter - m_t)
        s_mat = (jnp.exp(d_mat - m_t) * _dot_nt(q, k)).astype(BF16)
        intra = _dot(s_mat, v_aug)
        carried = _dot(q, c_prev.astype(BF16))
        num = w_inter * carried[:, :hd] + intra[:, :hd]
        den = w_inter * carried[:, hd:] + intra[:, hd:]
        hid = num / jnp.maximum(jnp.abs(den), jnp.exp(-m_t))

        mu = jnp.mean(hid, axis=-1, keepdims=True)
        cen = hid - mu
        var = jnp.mean(cen * cen, axis=-1, keepdims=True)
        normed = cen * lax.rsqrt(var + LN_EPS) * g_ref[:, hs]
        og = o_ref[:, hs].astype(F32)
        zg = z_ref[:, hs].astype(F32)
        out_ref[:, hs] = (normed * _sigmoid(og) * _silu(zg)).astype(BF16)

        b_last = b_col[cl - 1:cl, :]
        g_col = b_last - b_col + i_col
        m_new = jnp.maximum(b_last + m_prev, jnp.max(g_col, axis=0, keepdims=True))
        decay = jnp.exp(b_last + m_prev - m_new)
        w_k = jnp.exp(g_col - m_new)
        wv = (w_k * v_aug.astype(F32)).astype(BF16)
        c_scr[h] = decay * c_prev + _dot_tn(k, wv)
        m_scr[h] = jnp.broadcast_to(m_new, (1, LANES))


def _mlstm(main, ifg, norm_g, *, batch, seq):
    t = main.shape[0]
    cl = ML_CHUNK
    nc = seq // cl
    first = ML_Q0 // COL_GROUP
    blk = lambda g: pl.BlockSpec((cl, ML_WIDTH), lambda b, c: (b * nc + c, first + g))
    return pl.pallas_call(
        _mlstm_kernel,
        grid=(batch, nc),
        in_specs=[blk(0), blk(1), blk(2), blk(3), blk(4),
                  pl.BlockSpec((cl, LANES), lambda b, c: (b * nc + c, 0)),
                  pl.BlockSpec((1, ML_WIDTH), lambda b, c: (0, 0))],
        out_specs=pl.BlockSpec((cl, ML_WIDTH), lambda b, c: (b * nc + c, 0)),
        out_shape=jax.ShapeDtypeStruct((t, ML_WIDTH), BF16),
        scratch_shapes=[pltpu.VMEM((ML_HEADS, ML_HEAD_DIM, 2 * ML_HEAD_DIM), F32),
                        pltpu.VMEM((ML_HEADS, 1, LANES), F32)],
        compiler_params=pltpu.CompilerParams(
            dimension_semantics=("arbitrary", "arbitrary"), vmem_limit_bytes=VMEM_LIMIT_BYTES),
        name="mlstm",
    )(main, main, main, main, main, ifg, norm_g)


def _out_kernel(usb_ref, uml_ref, gate_ref, x_ref, mod_ref, wsb_ref, wml_ref, wo_ref,
                lng_ref, lnb_ref, o_ref, *, alpha):
    d = x_ref.shape[1]
    y_sb = _dot(usb_ref[...], wsb_ref[...])
    y_ml = _dot(uml_ref[...], wml_ref[...])
    mix = (_sigmoid(gate_ref[:, :d].astype(F32)) * y_sb
           + _sigmoid(gate_ref[:, d:].astype(F32)) * y_ml)
    y = _dot(mix.astype(BF16), wo_ref[...])
    gate = mod_ref[0, 2]
    r = alpha * x_ref[...] + (1.0 + gate) * y
    mu = jnp.mean(r, axis=-1, keepdims=True)
    cen = r - mu
    var = jnp.mean(cen * cen, axis=-1, keepdims=True)
    o_ref[...] = cen * lax.rsqrt(var + LN_EPS) * lng_ref[...] + lnb_ref[...]


def _out_proj(usb, uml, main, x2, mod, wsb, wml, wo, lng, lnb, *, seq, alpha):
    t, d = x2.shape
    tm = TOKEN_TILE
    tiles_per_seq = seq // tm
    const = lambda i: (0, 0)
    return pl.pallas_call(
        functools.partial(_out_kernel, alpha=alpha),
        grid=(t // tm,),
        in_specs=[
            pl.BlockSpec((tm, SB_WIDTH), lambda i: (i, 0)),
            pl.BlockSpec((tm, ML_WIDTH), lambda i: (i, 0)),
            pl.BlockSpec((tm, 2 * d), lambda i: (i, 0)),
            pl.BlockSpec((tm, d), lambda i: (i, 0)),
            pl.BlockSpec((1, 3, 1, d), lambda i: (i // tiles_per_seq, 0, 0, 0)),
            pl.BlockSpec((SB_WIDTH, d), const),
            pl.BlockSpec((ML_WIDTH, d), const),
            pl.BlockSpec((d, d), const),
            pl.BlockSpec((1, d), const),
            pl.BlockSpec((1, d), const),
        ],
        out_specs=pl.BlockSpec((tm, d), lambda i: (i, 0)),
        out_shape=jax.ShapeDtypeStruct((t, d), F32),
        compiler_params=pltpu.CompilerParams(
            dimension_semantics=("arbitrary",), vmem_limit_bytes=VMEM_LIMIT_BYTES),
        name="out_proj",
    )(usb, uml, main, x2, mod, wsb, wml, wo, lng, lnb)


def kernel(x, c, w_ada, b_ada, w_in, b_in, conv_w, conv_b, ml_norm_g, w_sb_proj, w_ml_proj, w_out, ln_g, ln_b):
    bsz, seq, d = x.shape
    depth = w_ada.shape[0]
    assert d == D_MODEL
    assert seq % TOKEN_TILE == 0 and seq % SB_TILE == 0 and seq % ML_CHUNK == 0
    alpha = (2.0 * depth) ** 0.25
    if0 = 4 * SB_WIDTH + 5 * ML_WIDTH
    g_start = if0 + 2 * ML_HEADS

    mod_all = _adaln(c, w_ada, b_ada).reshape(depth, bsz, 3, 1, d)
    x2 = x.reshape(bsz * seq, d)
    for l in range(depth):
        wm = jnp.concatenate([w_in[l, :, g_start:], w_in[l, :, :if0]], axis=1).astype(BF16)
        bm = jnp.concatenate([b_in[l, g_start:], b_in[l, :if0]])[None, :]
        pad = LANES - 2 * ML_HEADS
        wif = jnp.pad(w_in[l, :, if0:g_start], ((0, 0), (0, pad))).astype(BF16)
        bif = jnp.pad(b_in[l, if0:g_start], (0, pad))[None, :]
        main, ifg = _inproj(x2, mod_all[l], wm, bm, wif, bif, conv_w[l], conv_b[l][None, :], seq=seq)
        usb = _sb_attention(main, batch=bsz, seq=seq)
        uml = _mlstm(main, ifg, ml_norm_g[l][None, :], batch=bsz, seq=seq)
        x2 = _out_proj(usb, uml, main, x2, mod_all[l],
                       w_sb_proj[l].astype(BF16), w_ml_proj[l].astype(BF16), w_out[l].astype(BF16),
                       ln_g[l][None, :], ln_b[l][None, :], seq=seq, alpha=alpha)
    return x2.reshape(bsz, seq, d)
```

```python
import functools

import jax
import jax.numpy as jnp
from jax import lax
from jax.experimental import pallas as pl
from jax.experimental.pallas import tpu as pltpu

F32 = jnp.float32
BF16 = jnp.bfloat16

SB_HEADS = 8
SB_HEAD_DIM = 64
SB_WIDTH = SB_HEADS * SB_HEAD_DIM
ML_HEADS = 4
ML_HEAD_DIM = 128
ML_WIDTH = ML_HEADS * ML_HEAD_DIM
CONV_WIDTH = 4
LN_EPS = 1e-5

LANES = 128
SUBLANES = 8
VMEM_LIMIT_BYTES = 56 * 1024 * 1024

COL_GROUP = 512
D_MODEL = 1024
SB0 = 2 * D_MODEL
ML_Q0 = SB0 + 4 * SB_WIDTH

TOKEN_TILE = 512
SB_TILE = 256
SB_PAIRS_PER_STEP = 4
ML_CHUNK = 256

LOG2E = 1.4426950408889634
SB_LOG_CUTOFF = -120.0


def _dot(a, b):
    return jnp.dot(a, b, preferred_element_type=F32)


def _dot_nt(a, b):
    return lax.dot_general(a, b, (((1,), (1,)), ((), ())), preferred_element_type=F32)


def _dot_tn(a, b):
    return lax.dot_general(a, b, (((0,), (0,)), ((), ())), preferred_element_type=F32)


def _split_bf16(x):
    hi = x.astype(BF16)
    lo = (x - hi.astype(F32)).astype(BF16)
    return hi, lo


def _sigmoid(x):
    return 1.0 / (1.0 + jnp.exp(-x))


def _silu(x):
    return x * _sigmoid(x)


def _adaln_kernel(c_ref, w_ref, b_ref, o_ref):
    c = c_ref[...]
    o_ref[0] = _dot(_silu(c).astype(BF16), w_ref[0].astype(BF16)) + b_ref[0]


def _adaln(c, w_ada, b_ada):
    depth, d, _ = w_ada.shape
    bsz = c.shape[0]
    return pl.pallas_call(
        _adaln_kernel,
        grid=(depth, 3),
        in_specs=[
            pl.BlockSpec((bsz, d), lambda l, j: (0, 0)),
            pl.BlockSpec((1, d, d), lambda l, j: (l, 0, j)),
            pl.BlockSpec((1, 1, d), lambda l, j: (l, 0, j)),
        ],
        out_specs=pl.BlockSpec((1, bsz, d), lambda l, j: (l, 0, j)),
        out_shape=jax.ShapeDtypeStruct((depth, bsz, 3 * d), F32),
        compiler_params=pltpu.CompilerParams(
            dimension_semantics=("arbitrary", "arbitrary"), vmem_limit_bytes=VMEM_LIMIT_BYTES),
        name="adaln",
    )(c, w_ada, b_ada.reshape(depth, 1, 3 * d))


def _inproj_kernel(x_ref, xh_ref, mod_ref, w_ref, b_ref, wif_ref, bif_ref, cw_ref, cb_ref,
                   main_ref, if_ref, *, tiles_per_seq):
    first = (pl.program_id(0) % tiles_per_seq) == 0
    shift = mod_ref[0, 0]
    scale = mod_ref[0, 1]
    h = (x_ref[...] * (1.0 + scale) + shift).astype(BF16)
    h_halo = (xh_ref[...] * (1.0 + scale) + shift).astype(BF16)

    if_ref[...] = _dot(h, wif_ref[...]) + bif_ref[...]

    n_cols = main_ref.shape[1]
    for c0 in range(0, n_cols, COL_GROUP):
        cs = slice(c0, c0 + COL_GROUP)
        acc = _dot(h, w_ref[:, cs]) + b_ref[:, cs]
        if SB0 <= c0 < SB0 + SB_WIDTH:
            acc = acc * (SB_HEAD_DIM ** -0.5)
        elif ML_Q0 <= c0 < ML_Q0 + 2 * ML_WIDTH:
            ks = slice(c0 - ML_Q0, c0 - ML_Q0 + COL_GROUP)
            halo = _dot(h_halo, w_ref[:, cs]) + b_ref[:, cs]
            halo = jnp.where(first, 0.0, halo)
            cat = jnp.concatenate([halo, acc], axis=0)
            y = cb_ref[:, ks] + cw_ref[CONV_WIDTH - 1:CONV_WIDTH, ks] * acc
            for j in range(CONV_WIDTH - 1):
                shifted = pltpu.roll(cat, CONV_WIDTH - 1 - j, axis=0)[SUBLANES:, :]
                y = y + cw_ref[j:j + 1, ks] * shifted
            acc = _silu(y)
            if c0 < ML_Q0 + ML_WIDTH:
                acc = acc * (ML_HEAD_DIM ** -0.5)
        main_ref[:, cs] = acc.astype(BF16)


def _inproj(x2, mod, wm, bm, wif, bif, cw, cb, *, seq):
    t, d = x2.shape
    tm = TOKEN_TILE
    n_cols = wm.shape[1]
    tiles_per_seq = seq // tm
    const = lambda i: (0, 0)
    return pl.pallas_call(
        functools.partial(_inproj_kernel, tiles_per_seq=tiles_per_seq),
        grid=(t // tm,),
        in_specs=[
            pl.BlockSpec((tm, d), lambda i: (i, 0)),
            pl.BlockSpec((SUBLANES, d), lambda i: (jnp.maximum(i * (tm // SUBLANES) - 1, 0), 0)),
            pl.BlockSpec((1, 3, 1, d), lambda i: (i // tiles_per_seq, 0, 0, 0)),
            pl.BlockSpec((d, n_cols), const),
            pl.BlockSpec((1, n_cols), const),
            pl.BlockSpec((d, LANES), const),
            pl.BlockSpec((1, LANES), const),
            pl.BlockSpec((CONV_WIDTH, 2 * ML_WIDTH), const),
            pl.BlockSpec((1, 2 * ML_WIDTH), const),
        ],
        out_specs=[
            pl.BlockSpec((tm, n_cols), lambda i: (i, 0)),
            pl.BlockSpec((tm, LANES), lambda i: (i, 0)),
        ],
        out_shape=[
            jax.ShapeDtypeStruct((t, n_cols), BF16),
            jax.ShapeDtypeStruct((t, LANES), F32),
        ],
        compiler_params=pltpu.CompilerParams(
            dimension_semantics=("arbitrary",), vmem_limit_bytes=VMEM_LIMIT_BYTES),
        name="inproj",
    )(x2, x2, mod, wm, bm, wif, bif, cw, cb)


def _sb_kernel(q_ref, k_ref, v_ref, z_ref, o_ref, *, seq):
    ts = SB_TILE
    n_pairs = q_ref.shape[1] // LANES
    row = lax.broadcasted_iota(jnp.int32, (2 * ts, ts), 0)
    col = lax.broadcasted_iota(jnp.int32, (2 * ts, ts), 1)
    strict = col < (row & (ts - 1))
    upper = jnp.where(lax.broadcasted_iota(jnp.int32, (ts, ts), 0) > lax.broadcasted_iota(jnp.int32, (ts, ts), 1),
                      1.0, 0.0).astype(BF16)
    head0 = lax.broadcasted_iota(jnp.int32, (ts, LANES), 1) < SB_HEAD_DIM

    def q_body(i, _):
        q0 = pl.multiple_of(i * ts, ts)
        qs = []
        for p in range(n_pairs):
            q = q_ref[pl.ds(q0, ts), p * LANES:(p + 1) * LANES]
            zero = jnp.zeros_like(q)
            qs.append(jnp.concatenate([jnp.where(head0, q, zero), jnp.where(head0, zero, q)], axis=0))

        def key_tile(j0, carries, accs, diag):
            log_betas, log_rems = [], []
            for p in range(n_pairs):
                kj = k_ref[pl.ds(j0, ts), p * LANES:(p + 1) * LANES]
                z = _dot_nt(qs[p], kj)
                log_beta = jnp.minimum(z, 0.0) - jnp.log(1.0 + jnp.exp2(jnp.abs(z) * -LOG2E))
                log_rem = log_beta - z
                if diag:
                    log_rem = jnp.where(strict, log_rem, 0.0)
                log_betas.append(log_beta)
                log_rems.append(log_rem)
            hi, lo = _split_bf16(jnp.concatenate(log_rems, axis=0))
            suffix_all = _dot(hi, upper) + _dot(lo, upper)
            new_c, new_a = [], []
            for p in range(n_pairs):
                vj = v_ref[pl.ds(j0, ts), p * LANES:(p + 1) * LANES]
                suffix = suffix_all[2 * ts * p:2 * ts * (p + 1)]
                attn = jnp.exp((log_betas[p] + suffix) + carries[p])
                if diag:
                    attn = jnp.where(strict, attn, 0.0)
                new_a.append(accs[p] + _dot(attn.astype(BF16), vj))
                new_c.append(carries[p] + (suffix[:, 0:1] + log_rems[p][:, 0:1]))
            return tuple(new_c), tuple(new_a)

        def alive(carries):
            top = carries[0]
            for c in carries[1:]:
                top = jnp.maximum(top, c)
            return jnp.max(top) > SB_LOG_CUTOFF

        carries = tuple(jnp.zeros((2 * ts, 1), F32) for _ in range(n_pairs))
        accs = tuple(jnp.zeros((2 * ts, LANES), F32) for _ in range(n_pairs))
        carries, accs = key_tile(q0, carries, accs, True)

        def cond(st):
            jj, go, _, _ = st
            return jnp.logical_and(jj < i, go)

        def body(st):
            jj, _, carries, accs = st
            j0 = pl.multiple_of((i - 1 - jj) * ts, ts)
            carries, accs = key_tile(j0, carries, accs, False)
            return jj + 1, alive(carries), carries, accs

        _, _, carries, accs = lax.while_loop(cond, body, (jnp.int32(0), alive(carries), carries, accs))
        for p in range(n_pairs):
            ps = slice(p * LANES, (p + 1) * LANES)
            acc = jnp.where(head0, accs[p][:ts], accs[p][ts:])
            zg = z_ref[pl.ds(q0, ts), ps].astype(F32)
            o_ref[pl.ds(q0, ts), ps] = (acc * _silu(zg)).astype(BF16)
        return 0

    lax.fori_loop(0, seq // ts, q_body, 0)


def _sb_attention(main, *, batch, seq):
    t = main.shape[0]
    width = SB_PAIRS_PER_STEP * LANES
    n_steps = SB_WIDTH // width
    first = SB0 // width
    blk = lambda g: pl.BlockSpec((seq, width), lambda b, p: (b, first + g * n_steps + p))
    return pl.pallas_call(
        functools.partial(_sb_kernel, seq=seq),
        grid=(batch, n_steps),
        in_specs=[blk(0), blk(1), blk(2), blk(3)],
        out_specs=pl.BlockSpec((seq, width), lambda b, p: (b, p)),
        out_shape=jax.ShapeDtypeStruct((t, SB_WIDTH), BF16),
        compiler_params=pltpu.CompilerParams(
            dimension_semantics=("arbitrary", "arbitrary"), vmem_limit_bytes=VMEM_LIMIT_BYTES),
        name="sb_attention",
    )(main, main, main, main)


def _mlstm_kernel(q_ref, k_ref, v_ref, o_ref, z_ref, if_ref, g_ref, out_ref, c_scr, m_scr):
    cl = q_ref.shape[0]
    hd = ML_HEAD_DIM

    @pl.when(pl.program_id(1) == 0)
    def _():
        c_scr[...] = jnp.zeros_like(c_scr)
        m_scr[...] = jnp.zeros_like(m_scr)

    row = lax.broadcasted_iota(jnp.int32, (cl, cl), 0)
    col = lax.broadcasted_iota(jnp.int32, (cl, cl), 1)
    tril = col <= row
    tril_bf = jnp.where(tril, 1.0, 0.0).astype(BF16)

    gates = if_ref[...]
    log_f = jnp.minimum(gates, 0.0) - jnp.log(1.0 + jnp.exp(-jnp.abs(gates)))
    f_hi, f_lo = _split_bf16(log_f)
    b_all = _dot(tril_bf, f_hi) + _dot(tril_bf, f_lo)
    lane = lax.broadcasted_iota(jnp.int32, (cl, LANES), 1)
    rows_t = jnp.where(lane < ML_HEADS, gates, b_all).T
    ones = jnp.ones((cl, hd), BF16)

    for h in range(ML_HEADS):
        hs = slice(h * hd, (h + 1) * hd)
        q = q_ref[:, hs]
        k = k_ref[:, hs]
        v_aug = jnp.concatenate([v_ref[:, hs], ones], axis=1)
        b_col = b_all[:, ML_HEADS + h:ML_HEADS + h + 1]
        i_col = gates[:, h:h + 1]
        b_row = rows_t[ML_HEADS + h:ML_HEADS + h + 1, :]
        i_row = rows_t[h:h + 1, :]
        m_prev = m_scr[h][:, 0:1]
        c_prev = c_scr[h]

        d_mat = jnp.where(tril, b_col + (i_row - b_row), -jnp.inf)
        inter = b_col + m_prev
        m_t = jnp.maximum(inter, jnp.max(d_mat, axis=-1, keepdims=True))
        w_inter = jnp.exp(inter - m_t)
        s_mat = (jnp.exp(d_mat - m_t) * _dot_nt(q, k)).astype(BF16)
        intra = _dot(s_mat, v_aug)
        carried = _dot(q, c_prev.astype(BF16))
        num = w_inter * carried[:, :hd] + intra[:, :hd]
        den = w_inter * carried[:, hd:] + intra[:, hd:]
        hid = num / jnp.maximum(jnp.abs(den), jnp.exp(-m_t))

        mu = jnp.mean(hid, axis=-1, keepdims=True)
        cen = hid - mu
        var = jnp.mean(cen * cen, axis=-1, keepdims=True)
        normed = cen * lax.rsqrt(var + LN_EPS) * g_ref[:, hs]
        og = o_ref[:, hs].astype(F32)
        zg = z_ref[:, hs].astype(F32)
        out_ref[:, hs] = (normed * _sigmoid(og) * _silu(zg)).astype(BF16)

        b_last = b_col[cl - 1:cl, :]
        g_col = b_last - b_col + i_col
        m_new = jnp.maximum(b_last + m_prev, jnp.max(g_col, axis=0, keepdims=True))
        decay = jnp.exp(b_last + m_prev - m_new)
        w_k = jnp.exp(g_col - m_new)
        wv = (w_k * v_aug.astype(F32)).astype(BF16)
        c_scr[h] = decay * c_prev + _dot_tn(k, wv)
        m_scr[h] = jnp.broadcast_to(m_new, (1, LANES))


def _mlstm(main, ifg, norm_g, *, batch, seq):
    t = main.shape[0]
    cl = ML_CHUNK
    nc = seq // cl
    first = ML_Q0 // COL_GROUP
    blk = lambda g: pl.BlockSpec((cl, ML_WIDTH), lambda b, c: (b * nc + c, first + g))
    return pl.pallas_call(
        _mlstm_kernel,
        grid=(batch, nc),
        in_specs=[blk(0), blk(1), blk(2), blk(3), blk(4),
                  pl.BlockSpec((cl, LANES), lambda b, c: (b * nc + c, 0)),
                  pl.BlockSpec((1, ML_WIDTH), lambda b, c: (0, 0))],
        out_specs=pl.BlockSpec((cl, ML_WIDTH), lambda b, c: (b * nc + c, 0)),
        out_shape=jax.ShapeDtypeStruct((t, ML_WIDTH), BF16),
        scratch_shapes=[pltpu.VMEM((ML_HEADS, ML_HEAD_DIM, 2 * ML_HEAD_DIM), F32),
                        pltpu.VMEM((ML_HEADS, 1, LANES), F32)],
        compiler_params=pltpu.CompilerParams(
            dimension_semantics=("arbitrary", "arbitrary"), vmem_limit_bytes=VMEM_LIMIT_BYTES),
        name="mlstm",
    )(main, main, main, main, main, ifg, norm_g)


def _out_kernel(usb_ref, uml_ref, gate_ref, x_ref, mod_ref, wsb_ref, wml_ref, wo_ref,
                lng_ref, lnb_ref, o_ref, *, alpha):
    d = x_ref.shape[1]
    y_sb = _dot(usb_ref[...], wsb_ref[...])
    y_ml = _dot(uml_ref[...], wml_ref[...])
    mix = (_sigmoid(gate_ref[:, :d].astype(F32)) * y_sb
           + _sigmoid(gate_ref[:, d:].astype(F32)) * y_ml)
    y = _dot(mix.astype(BF16), wo_ref[...])
    gate = mod_ref[0, 2]
    r = alpha * x_ref[...] + (1.0 + gate) * y
    mu = jnp.mean(r, axis=-1, keepdims=True)
    cen = r - mu
    var = jnp.mean(cen * cen, axis=-1, keepdims=True)
    o_ref[...] = cen * lax.rsqrt(var + LN_EPS) * lng_ref[...] + lnb_ref[...]


def _out_proj(usb, uml, main, x2, mod, wsb, wml, wo, lng, lnb, *, seq, alpha):
    t, d = x2.shape
    tm = TOKEN_TILE
    tiles_per_seq = seq // tm
    const = lambda i: (0, 0)
    return pl.pallas_call(
        functools.partial(_out_kernel, alpha=alpha),
        grid=(t // tm,),
        in_specs=[
            pl.BlockSpec((tm, SB_WIDTH), lambda i: (i, 0)),
            pl.BlockSpec((tm, ML_WIDTH), lambda i: (i, 0)),
            pl.BlockSpec((tm, 2 * d), lambda i: (i, 0)),
            pl.BlockSpec((tm, d), lambda i: (i, 0)),
            pl.BlockSpec((1, 3, 1, d), lambda i: (i // tiles_per_seq, 0, 0, 0)),
            pl.BlockSpec((SB_WIDTH, d), const),
            pl.BlockSpec((ML_WIDTH, d), const),
            pl.BlockSpec((d, d), const),
            pl.BlockSpec((1, d), const),
            pl.BlockSpec((1, d), const),
        ],
        out_specs=pl.BlockSpec((tm, d), lambda i: (i, 0)),
        out_shape=jax.ShapeDtypeStruct((t, d), F32),
        compiler_params=pltpu.CompilerParams(
            dimension_semantics=("arbitrary",), vmem_limit_bytes=VMEM_LIMIT_BYTES),
        name="out_proj",
    )(usb, uml, main, x2, mod, wsb, wml, wo, lng, lnb)


def kernel(x, c, w_ada, b_ada, w_in, b_in, conv_w, conv_b, ml_norm_g, w_sb_proj, w_ml_proj, w_out, ln_g, ln_b):
    bsz, seq, d = x.shape
    depth = w_ada.shape[0]
    assert d == D_MODEL
    assert seq % TOKEN_TILE == 0 and seq % SB_TILE == 0 and seq % ML_CHUNK == 0
    alpha = (2.0 * depth) ** 0.25
    if0 = 4 * SB_WIDTH + 5 * ML_WIDTH
    g_start = if0 + 2 * ML_HEADS

    mod_all = _adaln(c, w_ada, b_ada).reshape(depth, bsz, 3, 1, d)
    x2 = x.reshape(bsz * seq, d)
    for l in range(depth):
        wm = jnp.concatenate([w_in[l, :, g_start:], w_in[l, :, :if0]], axis=1).astype(BF16)
        bm = jnp.concatenate([b_in[l, g_start:], b_in[l, :if0]])[None, :]
        pad = LANES - 2 * ML_HEADS
        wif = jnp.pad(w_in[l, :, if0:g_start], ((0, 0), (0, pad))).astype(BF16)
        bif = jnp.pad(b_in[l, if0:g_start], (0, pad))[None, :]
        main, ifg = _inproj(x2, mod_all[l], wm, bm, wif, bif, conv_w[l], conv_b[l][None, :], seq=seq)
        usb = _sb_attention(main, batch=bsz, seq=seq)
        uml = _mlstm(main, ifg, ml_norm_g[l][None, :], batch=bsz, seq=seq)
        x2 = _out_proj(usb, uml, main, x2, mod_all[l],
                       w_sb_proj[l].astype(BF16), w_ml_proj[l].astype(BF16), w_out[l].astype(BF16),
                       ln_g[l][None, :], ln_b[l][None, :], seq=seq, alpha=alpha)
    return x2.reshape(bsz, seq, d)
```

```python
import functools

import jax
import jax.numpy as jnp
from jax import lax
from jax.experimental import pallas as pl
from jax.experimental.pallas import tpu as pltpu

F32 = jnp.float32
BF16 = jnp.bfloat16

SB_HEADS = 8
SB_HEAD_DIM = 64
SB_WIDTH = SB_HEADS * SB_HEAD_DIM
ML_HEADS = 4
ML_HEAD_DIM = 128
ML_WIDTH = ML_HEADS * ML_HEAD_DIM
CONV_WIDTH = 4
LN_EPS = 1e-5

LANES = 128
SUBLANES = 8
VMEM_LIMIT_BYTES = 56 * 1024 * 1024

COL_GROUP = 512
D_MODEL = 1024
SB0 = 2 * D_MODEL
ML_Q0 = SB0 + 4 * SB_WIDTH

TOKEN_TILE = 512
HALO = 16
OUT_ROW_BLOCK = 256
SB_TILE = 256
SB_PAIRS_PER_STEP = 4
ML_CHUNK = 256

LOG2E = 1.4426950408889634
SB_LOG_CUTOFF = -120.0


def _dot(a, b):
    return jnp.dot(a, b, preferred_element_type=F32)


def _dot_nt(a, b):
    return lax.dot_general(a, b, (((1,), (1,)), ((), ())), preferred_element_type=F32)


def _dot_tn(a, b):
    return lax.dot_general(a, b, (((0,), (0,)), ((), ())), preferred_element_type=F32)


def _split_bf16(x):
    hi = x.astype(BF16)
    lo = (x - hi.astype(F32)).astype(BF16)
    return hi, lo


def _sigmoid(x):
    return 1.0 / (1.0 + jnp.exp(-x))


def _silu(x):
    return x * _sigmoid(x)


def _adaln_kernel(c_ref, w_ref, b_ref, o_ref):
    c = c_ref[...]
    o_ref[0] = _dot(_silu(c).astype(BF16), w_ref[0].astype(BF16)) + b_ref[0]


def _adaln(c, w_ada, b_ada):
    depth, d, _ = w_ada.shape
    bsz = c.shape[0]
    return pl.pallas_call(
        _adaln_kernel,
        grid=(depth, 3),
        in_specs=[
            pl.BlockSpec((bsz, d), lambda l, j: (0, 0)),
            pl.BlockSpec((1, d, d), lambda l, j: (l, 0, j)),
            pl.BlockSpec((1, 1, d), lambda l, j: (l, 0, j)),
        ],
        out_specs=pl.BlockSpec((1, bsz, d), lambda l, j: (l, 0, j)),
        out_shape=jax.ShapeDtypeStruct((depth, bsz, 3 * d), F32),
        compiler_params=pltpu.CompilerParams(
            dimension_semantics=("arbitrary", "arbitrary"), vmem_limit_bytes=VMEM_LIMIT_BYTES),
        name="adaln",
    )(c, w_ada, b_ada.reshape(depth, 1, 3 * d))


def _inproj_kernel(x_ref, xh_ref, mod_ref, w_ref, b_ref, wif_ref, bif_ref, cw_ref, cb_ref,
                   main_ref, if_ref, *, tiles_per_seq):
    first = (pl.program_id(0) % tiles_per_seq) == 0
    shift = mod_ref[0, 0]
    scale = mod_ref[0, 1]
    h = (x_ref[...] * (1.0 + scale) + shift).astype(BF16)
    h_halo = (xh_ref[...] * (1.0 + scale) + shift).astype(BF16)
    h_ext = jnp.concatenate([h_halo, h], axis=0)

    if_ref[...] = _dot(h, wif_ref[...]) + bif_ref[...]

    n_cols = main_ref.shape[1]
    for c0 in range(0, n_cols, COL_GROUP):
        cs = slice(c0, c0 + COL_GROUP)
        if SB0 <= c0 < SB0 + SB_WIDTH:
            acc = _dot(h, w_ref[:, cs]) + b_ref[:, cs]
            acc = acc * (SB_HEAD_DIM ** -0.5)
        elif ML_Q0 <= c0 < ML_Q0 + 2 * ML_WIDTH:
            ks = slice(c0 - ML_Q0, c0 - ML_Q0 + COL_GROUP)
            ext = _dot(h_ext, w_ref[:, cs]) + b_ref[:, cs]
            acc = ext[HALO:, :]
            halo = jnp.where(first, 0.0, ext[HALO - SUBLANES:HALO, :])
            cat = jnp.concatenate([halo, acc], axis=0)
            y = cb_ref[:, ks] + cw_ref[CONV_WIDTH - 1:CONV_WIDTH, ks] * acc
            for j in range(CONV_WIDTH - 1):
                shifted = pltpu.roll(cat, CONV_WIDTH - 1 - j, axis=0)[SUBLANES:, :]
                y = y + cw_ref[j:j + 1, ks] * shifted
            acc = _silu(y)
            if c0 < ML_Q0 + ML_WIDTH:
                acc = acc * (ML_HEAD_DIM ** -0.5)
        else:
            acc = _dot(h, w_ref[:, cs]) + b_ref[:, cs]
        main_ref[:, cs] = acc.astype(BF16)


def _inproj(x2, mod, wm, bm, wif, bif, cw, cb, *, seq):
    t, d = x2.shape
    tm = TOKEN_TILE
    n_cols = wm.shape[1]
    tiles_per_seq = seq // tm
    const = lambda i: (0, 0)
    return pl.pallas_call(
        functools.partial(_inproj_kernel, tiles_per_seq=tiles_per_seq),
        grid=(t // tm,),
        in_specs=[
            pl.BlockSpec((tm, d), lambda i: (i, 0)),
            pl.BlockSpec((HALO, d), lambda i: (jnp.maximum(i * (tm // HALO) - 1, 0), 0)),
            pl.BlockSpec((1, 3, 1, d), lambda i: (i // tiles_per_seq, 0, 0, 0)),
            pl.BlockSpec((d, n_cols), const),
            pl.BlockSpec((1, n_cols), const),
            pl.BlockSpec((d, LANES), const),
            pl.BlockSpec((1, LANES), const),
            pl.BlockSpec((CONV_WIDTH, 2 * ML_WIDTH), const),
            pl.BlockSpec((1, 2 * ML_WIDTH), const),
        ],
        out_specs=[
            pl.BlockSpec((tm, n_cols), lambda i: (i, 0)),
            pl.BlockSpec((tm, LANES), lambda i: (i, 0)),
        ],
        out_shape=[
            jax.ShapeDtypeStruct((t, n_cols), BF16),
            jax.ShapeDtypeStruct((t, LANES), F32),
        ],
        compiler_params=pltpu.CompilerParams(
            dimension_semantics=("arbitrary",), vmem_limit_bytes=VMEM_LIMIT_BYTES),
        name="inproj",
    )(x2, x2, mod, wm, bm, wif, bif, cw, cb)


def _sb_kernel(q_ref, k_ref, v_ref, z_ref, o_ref, *, seq):
    ts = SB_TILE
    n_pairs = q_ref.shape[1] // LANES
    row = lax.broadcasted_iota(jnp.int32, (2 * ts, ts), 0)
    col = lax.broadcasted_iota(jnp.int32, (2 * ts, ts), 1)
    strict = col < (row & (ts - 1))
    upper = jnp.where(lax.broadcasted_iota(jnp.int32, (ts, ts), 0) > lax.broadcasted_iota(jnp.int32, (ts, ts), 1),
                      1.0, 0.0).astype(BF16)
    upper2 = jnp.concatenate([upper, upper], axis=0)
    head0 = lax.broadcasted_iota(jnp.int32, (ts, LANES), 1) < SB_HEAD_DIM

    def q_body(i, _):
        q0 = pl.multiple_of(i * ts, ts)
        qs = []
        for p in range(n_pairs):
            q = q_ref[pl.ds(q0, ts), p * LANES:(p + 1) * LANES]
            zero = jnp.zeros_like(q)
            qs.append(jnp.concatenate([jnp.where(head0, q, zero), jnp.where(head0, zero, q)], axis=0))

        def key_tile(j0, carries, accs, diag):
            log_betas, log_rems = [], []
            for p in range(n_pairs):
                kj = k_ref[pl.ds(j0, ts), p * LANES:(p + 1) * LANES]
                z = _dot_nt(qs[p], kj)
                log_beta = jnp.minimum(z, 0.0) - jnp.log(1.0 + jnp.exp2(jnp.abs(z) * -LOG2E))
                log_rem = log_beta - z
                if diag:
                    log_rem = jnp.where(strict, log_rem, 0.0)
                log_betas.append(log_beta)
                log_rems.append(log_rem)
            hi, lo = _split_bf16(jnp.concatenate(log_rems, axis=0))
            suffix_all = _dot(jnp.concatenate([hi, lo], axis=1), upper2)
            new_c, new_a = [], []
            for p in range(n_pairs):
                vj = v_ref[pl.ds(j0, ts), p * LANES:(p + 1) * LANES]
                suffix = suffix_all[2 * ts * p:2 * ts * (p + 1)]
                attn = jnp.exp((log_betas[p] + suffix) + carries[p])
                if diag:
                    attn = jnp.where(strict, attn, 0.0)
                new_a.append(accs[p] + _dot(attn.astype(BF16), vj))
                new_c.append(carries[p] + (suffix[:, 0:1] + log_rems[p][:, 0:1]))
            return tuple(new_c), tuple(new_a)

        def alive(carries):
            top = carries[0]
            for c in carries[1:]:
                top = jnp.maximum(top, c)
            return jnp.max(top) > SB_LOG_CUTOFF

        carries = tuple(jnp.zeros((2 * ts, 1), F32) for _ in range(n_pairs))
        accs = tuple(jnp.zeros((2 * ts, LANES), F32) for _ in range(n_pairs))
        carries, accs = key_tile(q0, carries, accs, True)

        def cond(st):
            jj, go, _, _ = st
            return jnp.logical_and(jj < i, go)

        def body(st):
            jj, _, carries, accs = st
            j0 = pl.multiple_of((i - 1 - jj) * ts, ts)
            carries, accs = key_tile(j0, carries, accs, False)
            return jj + 1, alive(carries), carries, accs

        _, _, carries, accs = lax.while_loop(cond, body, (jnp.int32(0), alive(carries), carries, accs))
        for p in range(n_pairs):
            ps = slice(p * LANES, (p + 1) * LANES)
            acc = jnp.where(head0, accs[p][:ts], accs[p][ts:])
            zg = z_ref[pl.ds(q0, ts), ps].astype(F32)
            o_ref[pl.ds(q0, ts), ps] = (acc * _silu(zg)).astype(BF16)
        return 0

    lax.fori_loop(0, seq // ts, q_body, 0)


def _sb_attention(main, *, batch, seq):
    t = main.shape[0]
    width = SB_PAIRS_PER_STEP * LANES
    n_steps = SB_WIDTH // width
    first = SB0 // width
    blk = lambda g: pl.BlockSpec((seq, width), lambda b, p: (b, first + g * n_steps + p))
    return pl.pallas_call(
        functools.partial(_sb_kernel, seq=seq),
        grid=(batch, n_steps),
        in_specs=[blk(0), blk(1), blk(2), blk(3)],
        out_specs=pl.BlockSpec((seq, width), lambda b, p: (b, p)),
        out_shape=jax.ShapeDtypeStruct((t, SB_WIDTH), BF16),
        compiler_params=pltpu.CompilerParams(
            dimension_semantics=("arbitrary", "arbitrary"), vmem_limit_bytes=VMEM_LIMIT_BYTES),
        name="sb_attention",
    )(main, main, main, main)


def _mlstm_kernel(q_ref, k_ref, v_ref, o_ref, z_ref, if_ref, g_ref, out_ref, c_scr, m_scr):
    cl = q_ref.shape[0]
    hd = ML_HEAD_DIM

    @pl.when(pl.program_id(1) == 0)
    def _():
        c_scr[...] = jnp.zeros_like(c_scr)
        m_scr[...] = jnp.zeros_like(m_scr)

    row = lax.broadcasted_iota(jnp.int32, (cl, cl), 0)
    col = lax.broadcasted_iota(jnp.int32, (cl, cl), 1)
    tril = col <= row
    tril_bf = jnp.where(tril, 1.0, 0.0).astype(BF16)

    gates = if_ref[...]
    log_f = jnp.minimum(gates, 0.0) - jnp.log(1.0 + jnp.exp(-jnp.abs(gates)))
    f_hi, f_lo = _split_bf16(log_f)
    b_all = _dot(tril_bf, f_hi) + _dot(tril_bf, f_lo)
    lane = lax.broadcasted_iota(jnp.int32, (cl, LANES), 1)
    rows_t = jnp.where(lane < ML_HEADS, gates, b_all).T
    ones = jnp.ones((cl, hd), BF16)
    heads = range(ML_HEADS)
    hsl = [slice(h * hd, (h + 1) * hd) for h in heads]

    def wide(x):
        return jnp.concatenate([x, x], axis=1)

    q = [q_ref[:, hsl[h]] for h in heads]
    k = [k_ref[:, hsl[h]] for h in heads]
    v_aug = [jnp.concatenate([v_ref[:, hsl[h]], ones], axis=1) for h in heads]
    b_rep = [jnp.broadcast_to(b_all[:, ML_HEADS + h:ML_HEADS + h + 1], (cl, LANES)) for h in heads]
    i_rep = [jnp.broadcast_to(gates[:, h:h + 1], (cl, LANES)) for h in heads]
    m_prev = [m_scr[h] for h in heads]
    c_prev = [c_scr[h] for h in heads]

    qk = [_dot_nt(q[h], k[h]) for h in heads]
    carried = [_dot(q[h], c_prev[h].astype(BF16)) for h in heads]

    d_mat, m_t, inter = [], [], []
    for h in heads:
        r_row = rows_t[h:h + 1, :] - rows_t[ML_HEADS + h:ML_HEADS + h + 1, :]
        d = jnp.where(tril, wide(b_rep[h]) + r_row, -jnp.inf)
        it = b_rep[h] + m_prev[h]
        d_mat.append(d)
        inter.append(it)
        m_t.append(jnp.maximum(it, jnp.max(d, axis=-1, keepdims=True)))

    s_mat = [(jnp.exp(d_mat[h] - wide(m_t[h])) * qk[h]).astype(BF16) for h in heads]
    intra = [_dot(s_mat[h], v_aug[h]) for h in heads]

    for h in heads:
        b_last = b_rep[h][cl - 1:cl, :]
        g_rep = b_last - b_rep[h] + i_rep[h]
        m_new = jnp.maximum(b_last + m_prev[h], jnp.max(g_rep, axis=0, keepdims=True))
        decay = jnp.exp(b_last + m_prev[h] - m_new)
        w_k = jnp.exp(g_rep - m_new)
        wv = (wide(w_k) * v_aug[h].astype(F32)).astype(BF16)
        c_scr[h] = wide(decay) * c_prev[h] + _dot_tn(k[h], wv)
        m_scr[h] = m_new

    for h in heads:
        w_inter = jnp.exp(inter[h] - m_t[h])
        num = w_inter * carried[h][:, :hd] + intra[h][:, :hd]
        den = w_inter * carried[h][:, hd:] + intra[h][:, hd:]
        hid = num / jnp.maximum(jnp.abs(den), jnp.exp(-m_t[h]))
        mu = jnp.mean(hid, axis=-1, keepdims=True)
        cen = hid - mu
        var = jnp.mean(cen * cen, axis=-1, keepdims=True)
        normed = cen * lax.rsqrt(var + LN_EPS) * g_ref[:, hsl[h]]
        og = o_ref[:, hsl[h]].astype(F32)
        zg = z_ref[:, hsl[h]].astype(F32)
        out_ref[:, hsl[h]] = (normed * _sigmoid(og) * _silu(zg)).astype(BF16)


def _mlstm(main, ifg, norm_g, *, batch, seq):
    t = main.shape[0]
    cl = ML_CHUNK
    nc = seq // cl
    first = ML_Q0 // COL_GROUP
    blk = lambda g: pl.BlockSpec((cl, ML_WIDTH), lambda b, c: (b * nc + c, first + g))
    return pl.pallas_call(
        _mlstm_kernel,
        grid=(batch, nc),
        in_specs=[blk(0), blk(1), blk(2), blk(3), blk(4),
                  pl.BlockSpec((cl, LANES), lambda b, c: (b * nc + c, 0)),
                  pl.BlockSpec((1, ML_WIDTH), lambda b, c: (0, 0))],
        out_specs=pl.BlockSpec((cl, ML_WIDTH), lambda b, c: (b * nc + c, 0)),
        out_shape=jax.ShapeDtypeStruct((t, ML_WIDTH), BF16),
        scratch_shapes=[pltpu.VMEM((ML_HEADS, ML_HEAD_DIM, 2 * ML_HEAD_DIM), F32),
                        pltpu.VMEM((ML_HEADS, 1, LANES), F32)],
        compiler_params=pltpu.CompilerParams(
            dimension_semantics=("arbitrary", "arbitrary"), vmem_limit_bytes=VMEM_LIMIT_BYTES),
        name="mlstm",
    )(main, main, main, main, main, ifg, norm_g)


def _out_kernel(usb_ref, uml_ref, gate_ref, x_ref, mod_ref, wsb_ref, wml_ref, wo_ref,
                lng_ref, lnb_ref, o_ref, *, alpha):
    tm, d = x_ref.shape
    gate = mod_ref[0, 2]
    blocks = [slice(r0, r0 + OUT_ROW_BLOCK) for r0 in range(0, tm, OUT_ROW_BLOCK)]
    y_sb = [_dot(usb_ref[rs, :], wsb_ref[...]) for rs in blocks]
    y_ml = [_dot(uml_ref[rs, :], wml_ref[...]) for rs in blocks]
    mix = [(_sigmoid(gate_ref[rs, :d].astype(F32)) * y_sb[n]
            + _sigmoid(gate_ref[rs, d:].astype(F32)) * y_ml[n]).astype(BF16) for n, rs in enumerate(blocks)]
    y = [_dot(mix[n], wo_ref[...]) for n in range(len(blocks))]
    for n, rs in enumerate(blocks):
        r = alpha * x_ref[rs, :] + (1.0 + gate) * y[n]
        mu = jnp.mean(r, axis=-1, keepdims=True)
        cen = r - mu
        var = jnp.mean(cen * cen, axis=-1, keepdims=True)
        o_ref[rs, :] = cen * lax.rsqrt(var + LN_EPS) * lng_ref[...] + lnb_ref[...]


def _out_proj(usb, uml, main, x2, mod, wsb, wml, wo, lng, lnb, *, seq, alpha):
    t, d = x2.shape
    tm = TOKEN_TILE
    tiles_per_seq = seq // tm
    const = lambda i: (0, 0)
    return pl.pallas_call(
        functools.partial(_out_kernel, alpha=alpha),
        grid=(t // tm,),
        in_specs=[
            pl.BlockSpec((tm, SB_WIDTH), lambda i: (i, 0)),
            pl.BlockSpec((tm, ML_WIDTH), lambda i: (i, 0)),
            pl.BlockSpec((tm, 2 * d), lambda i: (i, 0)),
            pl.BlockSpec((tm, d), lambda i: (i, 0)),
            pl.BlockSpec((1, 3, 1, d), lambda i: (i // tiles_per_seq, 0, 0, 0)),
            pl.BlockSpec((SB_WIDTH, d), const),
            pl.BlockSpec((ML_WIDTH, d), const),
            pl.BlockSpec((d, d), const),
            pl.BlockSpec((1, d), const),
            pl.BlockSpec((1, d), const),
        ],
        out_specs=pl.BlockSpec((tm, d), lambda i: (i, 0)),
        out_shape=jax.ShapeDtypeStruct((t, d), F32),
        compiler_params=pltpu.CompilerParams(
            dimension_semantics=("arbitrary",), vmem_limit_bytes=VMEM_LIMIT_BYTES),
        name="out_proj",
    )(usb, uml, main, x2, mod, wsb, wml, wo, lng, lnb)


def kernel(x, c, w_ada, b_ada, w_in, b_in, conv_w, conv_b, ml_norm_g, w_sb_proj, w_ml_proj, w_out, ln_g, ln_b):
    bsz, seq, d = x.shape
    depth = w_ada.shape[0]
    assert d == D_MODEL
    assert seq % TOKEN_TILE == 0 and seq % SB_TILE == 0 and seq % ML_CHUNK == 0
    alpha = (2.0 * depth) ** 0.25
    if0 = 4 * SB_WIDTH + 5 * ML_WIDTH
    g_start = if0 + 2 * ML_HEADS

    mod_all = _adaln(c, w_ada, b_ada).reshape(depth, bsz, 3, 1, d)
    x2 = x.reshape(bsz * seq, d)
    for l in range(depth):
        wm = jnp.concatenate([w_in[l, :, g_start:], w_in[l, :, :if0]], axis=1).astype(BF16)
        bm = jnp.concatenate([b_in[l, g_start:], b_in[l, :if0]])[None, :]
        pad = LANES - 2 * ML_HEADS
        wif = jnp.pad(w_in[l, :, if0:g_start], ((0, 0), (0, pad))).astype(BF16)
        bif = jnp.pad(b_in[l, if0:g_start], (0, pad))[None, :]
        main, ifg = _inproj(x2, mod_all[l], wm, bm, wif, bif, conv_w[l], conv_b[l][None, :], seq=seq)
        usb = _sb_attention(main, batch=bsz, seq=seq)
        uml = _mlstm(main, ifg, ml_norm_g[l][None, :], batch=bsz, seq=seq)
        x2 = _out_proj(usb, uml, main, x2, mod_all[l],
                       w_sb_proj[l].astype(BF16), w_ml_proj[l].astype(BF16), w_out[l].astype(BF16),
                       ln_g[l][None, :], ln_b[l][None, :], seq=seq, alpha=alpha)
    return x2.reshape(bsz, seq, d)
```

```python
import functools

import jax
import jax.numpy as jnp
from jax import lax
from jax.experimental import pallas as pl
from jax.experimental.pallas import tpu as pltpu

F32 = jnp.float32
BF16 = jnp.bfloat16

SB_HEADS = 8
SB_HEAD_DIM = 64
SB_WIDTH = SB_HEADS * SB_HEAD_DIM
ML_HEADS = 4
ML_HEAD_DIM = 128
ML_WIDTH = ML_HEADS * ML_HEAD_DIM
CONV_WIDTH = 4
LN_EPS = 1e-5

LANES = 128
SUBLANES = 8
VMEM_LIMIT_BYTES = 56 * 1024 * 1024

COL_GROUP = 512
D_MODEL = 1024
SB0 = 2 * D_MODEL
ML_Q0 = SB0 + 4 * SB_WIDTH

TOKEN_TILE = 512
HALO = 16
OUT_ROW_BLOCK = 256
SB_TILE = 256
SB_PAIRS_PER_STEP = 4
ML_CHUNK = 256

LOG2E = 1.4426950408889634
SB_LOG_CUTOFF = -120.0


def _dot(a, b):
    return jnp.dot(a, b, preferred_element_type=F32)


def _dot_nt(a, b):
    return lax.dot_general(a, b, (((1,), (1,)), ((), ())), preferred_element_type=F32)


def _dot_tn(a, b):
    return lax.dot_general(a, b, (((0,), (0,)), ((), ())), preferred_element_type=F32)


def _split_bf16(x):
    hi = x.astype(BF16)
    lo = (x - hi.astype(F32)).astype(BF16)
    return hi, lo


def _sigmoid(x):
    return jax.nn.sigmoid(x)


def _silu(x):
    return x * _sigmoid(x)


def _adaln_kernel(c_ref, w_ref, b_ref, o_ref):
    c = c_ref[...]
    o_ref[0] = _dot(_silu(c).astype(BF16), w_ref[0].astype(BF16)) + b_ref[0]


def _adaln(c, w_ada, b_ada):
    depth, d, _ = w_ada.shape
    bsz = c.shape[0]
    return pl.pallas_call(
        _adaln_kernel,
        grid=(depth, 3),
        in_specs=[
            pl.BlockSpec((bsz, d), lambda l, j: (0, 0)),
            pl.BlockSpec((1, d, d), lambda l, j: (l, 0, j)),
            pl.BlockSpec((1, 1, d), lambda l, j: (l, 0, j)),
        ],
        out_specs=pl.BlockSpec((1, bsz, d), lambda l, j: (l, 0, j)),
        out_shape=jax.ShapeDtypeStruct((depth, bsz, 3 * d), F32),
        compiler_params=pltpu.CompilerParams(
            dimension_semantics=("arbitrary", "arbitrary"), vmem_limit_bytes=VMEM_LIMIT_BYTES),
        name="adaln",
    )(c, w_ada, b_ada.reshape(depth, 1, 3 * d))


def _mlstm_chunk(q, k, v, og, zg, gates, norm_g, c_prev, m_prev, emit):
    cl = q.shape[0]
    hd = ML_HEAD_DIM
    heads = range(ML_HEADS)
    hsl = [slice(h * hd, (h + 1) * hd) for h in heads]
    row = lax.broadcasted_iota(jnp.int32, (cl, cl), 0)
    col = lax.broadcasted_iota(jnp.int32, (cl, cl), 1)
    tril = col <= row
    tril_bf = jnp.where(tril, 1.0, 0.0).astype(BF16)

    def wide(x):
        return jnp.concatenate([x, x], axis=1)

    log_f = jnp.minimum(gates, 0.0) - jnp.log(1.0 + jnp.exp(-jnp.abs(gates)))
    f_hi, f_lo = _split_bf16(log_f)
    b_all = _dot(tril_bf, f_hi) + _dot(tril_bf, f_lo)
    lane = lax.broadcasted_iota(jnp.int32, (cl, LANES), 1)
    rows_t = jnp.where(lane < ML_HEADS, gates, b_all).T
    ones = jnp.ones((cl, hd), BF16)

    qs = [q[:, hsl[h]] for h in heads]
    ks = [k[:, hsl[h]] for h in heads]
    v_aug = [jnp.concatenate([v[:, hsl[h]], ones], axis=1) for h in heads]
    b_rep = [jnp.broadcast_to(b_all[:, ML_HEADS + h:ML_HEADS + h + 1], (cl, LANES)) for h in heads]
    i_rep = [jnp.broadcast_to(gates[:, h:h + 1], (cl, LANES)) for h in heads]

    qk = [_dot_nt(qs[h], ks[h]) for h in heads]
    carried = [_dot(qs[h], c_prev[h].astype(BF16)) for h in heads]

    d_mat, m_t, inter = [], [], []
    for h in heads:
        r_row = rows_t[h:h + 1, :] - rows_t[ML_HEADS + h:ML_HEADS + h + 1, :]
        d = jnp.where(tril, wide(b_rep[h]) + r_row, -jnp.inf)
        it = b_rep[h] + m_prev[h]
        d_mat.append(d)
        inter.append(it)
        m_t.append(jnp.maximum(it, jnp.max(d, axis=-1, keepdims=True)))
    emit()

    s_mat = [(jnp.exp(d_mat[h] - wide(m_t[h])) * qk[h]).astype(BF16) for h in heads]
    intra = [_dot(s_mat[h], v_aug[h]) for h in heads]

    c_new, m_new = [], []
    for h in heads:
        b_last = b_rep[h][cl - 1:cl, :]
        g_rep = b_last - b_rep[h] + i_rep[h]
        m_n = jnp.maximum(b_last + m_prev[h], jnp.max(g_rep, axis=0, keepdims=True))
        decay = jnp.exp(b_last + m_prev[h] - m_n)
        w_k = jnp.exp(g_rep - m_n)
        wv = (wide(w_k) * v_aug[h].astype(F32)).astype(BF16)
        c_new.append(wide(decay) * c_prev[h] + _dot_tn(ks[h], wv))
        m_new.append(m_n)
    emit()

    outs = []
    for h in heads:
        w_inter = jnp.exp(inter[h] - m_t[h])
        num = w_inter * carried[h][:, :hd] + intra[h][:, :hd]
        den = w_inter * carried[h][:, hd:] + intra[h][:, hd:]
        hid = num / jnp.maximum(jnp.abs(den), jnp.exp(-m_t[h]))
        mu = jnp.mean(hid, axis=-1, keepdims=True)
        cen = hid - mu
        var = jnp.mean(cen * cen, axis=-1, keepdims=True)
        normed = cen * lax.rsqrt(var + LN_EPS) * norm_g[:, hsl[h]]
        o_h = og[:, hsl[h]].astype(F32)
        z_h = zg[:, hsl[h]].astype(F32)
        outs.append((normed * _sigmoid(o_h) * _silu(z_h)).astype(BF16))
    return outs, c_new, m_new


def _inproj_kernel(x_ref, xh_ref, mod_ref, w_ref, b_ref, wif_ref, bif_ref, cw_ref, cb_ref, ng_ref,
                   main_ref, uml_ref, c_scr, m_scr, *, tiles_per_seq):
    first = (pl.program_id(0) % tiles_per_seq) == 0

    @pl.when(first)
    def _():
        c_scr[...] = jnp.zeros_like(c_scr)
        m_scr[...] = jnp.zeros_like(m_scr)

    tm = x_ref.shape[0]
    shift = mod_ref[0, 0]
    scale = mod_ref[0, 1]
    h = (x_ref[...] * (1.0 + scale) + shift).astype(BF16)
    h_halo = (xh_ref[...] * (1.0 + scale) + shift).astype(BF16)
    h_ext = jnp.concatenate([h_halo, h], axis=0)

    def column_group(c0):
        cs = slice(c0, c0 + COL_GROUP)
        if ML_Q0 <= c0 < ML_Q0 + 2 * ML_WIDTH:
            ks = slice(c0 - ML_Q0, c0 - ML_Q0 + COL_GROUP)
            ext = _dot(h_ext, w_ref[:, cs]) + b_ref[:, cs]
            acc = ext[HALO:, :]
            halo = jnp.where(first, 0.0, ext[HALO - SUBLANES:HALO, :])
            cat = jnp.concatenate([halo, acc], axis=0)
            y = cb_ref[:, ks] + cw_ref[CONV_WIDTH - 1:CONV_WIDTH, ks] * acc
            for j in range(CONV_WIDTH - 1):
                shifted = pltpu.roll(cat, CONV_WIDTH - 1 - j, axis=0)[SUBLANES:, :]
                y = y + cw_ref[j:j + 1, ks] * shifted
            acc = _silu(y)
            if c0 < ML_Q0 + ML_WIDTH:
                acc = acc * (ML_HEAD_DIM ** -0.5)
            return acc
        acc = _dot(h, w_ref[:, cs]) + b_ref[:, cs]
        if SB0 <= c0 < SB0 + SB_WIDTH:
            acc = acc * (SB_HEAD_DIM ** -0.5)
        return acc

    ml = [column_group(ML_Q0 + g * ML_WIDTH).astype(BF16) for g in range(5)]
    gates = _dot(h, wif_ref[...]) + bif_ref[...]

    pending = list(range(0, main_ref.shape[1], COL_GROUP))
    n_chunks = tm // ML_CHUNK
    per_emit = -(-len(pending) // (2 * n_chunks))

    def emit():
        for _ in range(min(per_emit, len(pending))):
            c0 = pending.pop(0)
            main_ref[:, c0:c0 + COL_GROUP] = column_group(c0).astype(BF16)

    c_state = [c_scr[hh] for hh in range(ML_HEADS)]
    m_state = [m_scr[hh] for hh in range(ML_HEADS)]
    norm_g = ng_ref[...]
    for n in range(n_chunks):
        rs = slice(n * ML_CHUNK, (n + 1) * ML_CHUNK)
        outs, c_state, m_state = _mlstm_chunk(ml[0][rs], ml[1][rs], ml[2][rs], ml[3][rs], ml[4][rs],
                                              gates[rs], norm_g, c_state, m_state, emit)
        for hh in range(ML_HEADS):
            uml_ref[rs, hh * ML_HEAD_DIM:(hh + 1) * ML_HEAD_DIM] = outs[hh]
    while pending:
        emit()
    for hh in range(ML_HEADS):
        c_scr[hh] = c_state[hh]
        m_scr[hh] = m_state[hh]


def _inproj(x2, mod, wm, bm, wif, bif, cw, cb, norm_g, *, seq):
    t, d = x2.shape
    tm = TOKEN_TILE
    n_cols = wm.shape[1]
    tiles_per_seq = seq // tm
    const = lambda i: (0, 0)
    return pl.pallas_call(
        functools.partial(_inproj_kernel, tiles_per_seq=tiles_per_seq),
        grid=(t // tm,),
        in_specs=[
            pl.BlockSpec((tm, d), lambda i: (i, 0)),
            pl.BlockSpec((HALO, d), lambda i: (jnp.maximum(i * (tm // HALO) - 1, 0), 0)),
            pl.BlockSpec((1, 3, 1, d), lambda i: (i // tiles_per_seq, 0, 0, 0)),
            pl.BlockSpec((d, n_cols), const),
            pl.BlockSpec((1, n_cols), const),
            pl.BlockSpec((d, LANES), const),
            pl.BlockSpec((1, LANES), const),
            pl.BlockSpec((CONV_WIDTH, 2 * ML_WIDTH), const),
            pl.BlockSpec((1, 2 * ML_WIDTH), const),
            pl.BlockSpec((1, ML_WIDTH), const),
        ],
        out_specs=[
            pl.BlockSpec((tm, ML_Q0), lambda i: (i, 0)),
            pl.BlockSpec((tm, ML_WIDTH), lambda i: (i, 0)),
        ],
        out_shape=[
            jax.ShapeDtypeStruct((t, ML_Q0), BF16),
            jax.ShapeDtypeStruct((t, ML_WIDTH), BF16),
        ],
        scratch_shapes=[pltpu.VMEM((ML_HEADS, ML_HEAD_DIM, 2 * ML_HEAD_DIM), F32),
                        pltpu.VMEM((ML_HEADS, 1, LANES), F32)],
        compiler_params=pltpu.CompilerParams(
            dimension_semantics=("arbitrary",), vmem_limit_bytes=VMEM_LIMIT_BYTES),
        name="inproj_mlstm",
    )(x2, x2, mod, wm, bm, wif, bif, cw, cb, norm_g)


def _sb_kernel(q_ref, k_ref, v_ref, z_ref, o_ref, *, seq):
    ts = SB_TILE
    n_pairs = q_ref.shape[1] // LANES
    row = lax.broadcasted_iota(jnp.int32, (2 * ts, ts), 0)
    col = lax.broadcasted_iota(jnp.int32, (2 * ts, ts), 1)
    strict = col < (row & (ts - 1))
    upper = jnp.where(lax.broadcasted_iota(jnp.int32, (ts, ts), 0) > lax.broadcasted_iota(jnp.int32, (ts, ts), 1),
                      1.0, 0.0).astype(BF16)
    upper2 = jnp.concatenate([upper, upper], axis=0)
    head0 = lax.broadcasted_iota(jnp.int32, (ts, LANES), 1) < SB_HEAD_DIM

    def q_body(i, _):
        q0 = pl.multiple_of(i * ts, ts)
        qs = []
        for p in range(n_pairs):
            q = q_ref[pl.ds(q0, ts), p * LANES:(p + 1) * LANES]
            zero = jnp.zeros_like(q)
            qs.append(jnp.concatenate([jnp.where(head0, q, zero), jnp.where(head0, zero, q)], axis=0))

        def key_tile(j0, carries, accs, diag):
            log_betas, log_rems = [], []
            for p in range(n_pairs):
                kj = k_ref[pl.ds(j0, ts), p * LANES:(p + 1) * LANES]
                z = _dot_nt(qs[p], kj)
                log_beta = jnp.minimum(z, 0.0) - jnp.log(1.0 + jnp.exp2(jnp.abs(z) * -LOG2E))
                log_rem = log_beta - z
                if diag:
                    log_rem = jnp.where(strict, log_rem, 0.0)
                log_betas.append(log_beta)
                log_rems.append(log_rem)
            hi, lo = _split_bf16(jnp.concatenate(log_rems, axis=0))
            suffix_all = _dot(jnp.concatenate([hi, lo], axis=1), upper2)
            new_c, new_a = [], []
            for p in range(n_pairs):
                vj = v_ref[pl.ds(j0, ts), p * LANES:(p + 1) * LANES]
                suffix = suffix_all[2 * ts * p:2 * ts * (p + 1)]
                attn = jnp.exp((log_betas[p] + suffix) + carries[p])
                if diag:
                    attn = jnp.where(strict, attn, 0.0)
                new_a.append(accs[p] + _dot(attn.astype(BF16), vj))
                new_c.append(carries[p] + (suffix[:, 0:1] + log_rems[p][:, 0:1]))
            return tuple(new_c), tuple(new_a)

        def alive(carries):
            top = carries[0]
            for c in carries[1:]:
                top = jnp.maximum(top, c)
            return jnp.max(top) > SB_LOG_CUTOFF

        carries = tuple(jnp.zeros((2 * ts, 1), F32) for _ in range(n_pairs))
        accs = tuple(jnp.zeros((2 * ts, LANES), F32) for _ in range(n_pairs))
        carries, accs = key_tile(q0, carries, accs, True)

        def cond(st):
            jj, go, _, _ = st
            return jnp.logical_and(jj < i, go)

        def body(st):
            jj, _, carries, accs = st
            j0 = pl.multiple_of((i - 1 - jj) * ts, ts)
            carries, accs = key_tile(j0, carries, accs, False)
            return jj + 1, alive(carries), carries, accs

        _, _, carries, accs = lax.while_loop(cond, body, (jnp.int32(0), alive(carries), carries, accs))
        for p in range(n_pairs):
            ps = slice(p * LANES, (p + 1) * LANES)
            acc = jnp.where(head0, accs[p][:ts], accs[p][ts:])
            zg = z_ref[pl.ds(q0, ts), ps].astype(F32)
            o_ref[pl.ds(q0, ts), ps] = (acc * _silu(zg)).astype(BF16)
        return 0

    lax.fori_loop(0, seq // ts, q_body, 0)


def _sb_attention(main, *, batch, seq):
    t = main.shape[0]
    width = SB_PAIRS_PER_STEP * LANES
    n_steps = SB_WIDTH // width
    first = SB0 // width
    blk = lambda g: pl.BlockSpec((seq, width), lambda b, p: (b, first + g * n_steps + p))
    return pl.pallas_call(
        functools.partial(_sb_kernel, seq=seq),
        grid=(batch, n_steps),
        in_specs=[blk(0), blk(1), blk(2), blk(3)],
        out_specs=pl.BlockSpec((seq, width), lambda b, p: (b, p)),
        out_shape=jax.ShapeDtypeStruct((t, SB_WIDTH), BF16),
        compiler_params=pltpu.CompilerParams(
            dimension_semantics=("arbitrary", "arbitrary"), vmem_limit_bytes=VMEM_LIMIT_BYTES),
        name="sb_attention",
    )(main, main, main, main)


def _out_kernel(usb_ref, uml_ref, gate_ref, x_ref, mod_ref, wsb_ref, wml_ref, wo_ref,
                lng_ref, lnb_ref, o_ref, *, alpha):
    tm, d = x_ref.shape
    gate = mod_ref[0, 2]
    blocks = [slice(r0, r0 + OUT_ROW_BLOCK) for r0 in range(0, tm, OUT_ROW_BLOCK)]
    y_sb = [_dot(usb_ref[rs, :], wsb_ref[...]) for rs in blocks]
    y_ml = [_dot(uml_ref[rs, :], wml_ref[...]) for rs in blocks]
    mix = [(_sigmoid(gate_ref[rs, :d].astype(F32)) * y_sb[n]
            + _sigmoid(gate_ref[rs, d:].astype(F32)) * y_ml[n]).astype(BF16) for n, rs in enumerate(blocks)]
    y = [_dot(mix[n], wo_ref[...]) for n in range(len(blocks))]
    for n, rs in enumerate(blocks):
        r = alpha * x_ref[rs, :] + (1.0 + gate) * y[n]
        mu = jnp.mean(r, axis=-1, keepdims=True)
        cen = r - mu
        var = jnp.mean(cen * cen, axis=-1, keepdims=True)
        o_ref[rs, :] = cen * lax.rsqrt(var + LN_EPS) * lng_ref[...] + lnb_ref[...]


def _out_proj(usb, uml, main, x2, mod, wsb, wml, wo, lng, lnb, *, seq, alpha):
    t, d = x2.shape
    tm = TOKEN_TILE
    tiles_per_seq = seq // tm
    const = lambda i: (0, 0)
    return pl.pallas_call(
        functools.partial(_out_kernel, alpha=alpha),
        grid=(t // tm,),
        in_specs=[
            pl.BlockSpec((tm, SB_WIDTH), lambda i: (i, 0)),
            pl.BlockSpec((tm, ML_WIDTH), lambda i: (i, 0)),
            pl.BlockSpec((tm, 2 * d), lambda i: (i, 0)),
            pl.BlockSpec((tm, d), lambda i: (i, 0)),
            pl.BlockSpec((1, 3, 1, d), lambda i: (i // tiles_per_seq, 0, 0, 0)),
            pl.BlockSpec((SB_WIDTH, d), const),
            pl.BlockSpec((ML_WIDTH, d), const),
            pl.BlockSpec((d, d), const),
            pl.BlockSpec((1, d), const),
            pl.BlockSpec((1, d), const),
        ],
        out_specs=pl.BlockSpec((tm, d), lambda i: (i, 0)),
        out_shape=jax.ShapeDtypeStruct((t, d), F32),
        compiler_params=pltpu.CompilerParams(
            dimension_semantics=("arbitrary",), vmem_limit_bytes=VMEM_LIMIT_BYTES),
        name="out_proj",
    )(usb, uml, main, x2, mod, wsb, wml, wo, lng, lnb)


def kernel(x, c, w_ada, b_ada, w_in, b_in, conv_w, conv_b, ml_norm_g, w_sb_proj, w_ml_proj, w_out, ln_g, ln_b):
    bsz, seq, d = x.shape
    depth = w_ada.shape[0]
    assert d == D_MODEL
    assert seq % TOKEN_TILE == 0 and seq % SB_TILE == 0 and seq % ML_CHUNK == 0
    alpha = (2.0 * depth) ** 0.25
    if0 = 4 * SB_WIDTH + 5 * ML_WIDTH
    g_start = if0 + 2 * ML_HEADS

    mod_all = _adaln(c, w_ada, b_ada).reshape(depth, bsz, 3, 1, d)
    x2 = x.reshape(bsz * seq, d)
    for l in range(depth):
        wm = jnp.concatenate([w_in[l, :, g_start:], w_in[l, :, :if0]], axis=1).astype(BF16)
        bm = jnp.concatenate([b_in[l, g_start:], b_in[l, :if0]])[None, :]
        pad = LANES - 2 * ML_HEADS
        wif = jnp.pad(w_in[l, :, if0:g_start], ((0, 0), (0, pad))).astype(BF16)
        bif = jnp.pad(b_in[l, if0:g_start], (0, pad))[None, :]
        main, uml = _inproj(x2, mod_all[l], wm, bm, wif, bif, conv_w[l], conv_b[l][None, :],
                            ml_norm_g[l][None, :], seq=seq)
        usb = _sb_attention(main, batch=bsz, seq=seq)
        x2 = _out_proj(usb, uml, main, x2, mod_all[l],
                       w_sb_proj[l].astype(BF16), w_ml_proj[l].astype(BF16), w_out[l].astype(BF16),
                       ln_g[l][None, :], ln_b[l][None, :], seq=seq, alpha=alpha)
    return x2.reshape(bsz, seq, d)
```

```python
import functools

import jax
import jax.numpy as jnp
from jax import lax
from jax.experimental import pallas as pl
from jax.experimental.pallas import tpu as pltpu

F32 = jnp.float32
BF16 = jnp.bfloat16

SB_HEADS = 8
SB_HEAD_DIM = 64
SB_WIDTH = SB_HEADS * SB_HEAD_DIM
ML_HEADS = 4
ML_HEAD_DIM = 128
ML_WIDTH = ML_HEADS * ML_HEAD_DIM
CONV_WIDTH = 4
LN_EPS = 1e-5

LANES = 128
SUBLANES = 8
VMEM_LIMIT_BYTES = 56 * 1024 * 1024

COL_GROUP = 512
D_MODEL = 1024
SB0 = 2 * D_MODEL
ML_Q0 = SB0 + 4 * SB_WIDTH

TOKEN_TILE = 512
HALO = 16
OUT_ROW_BLOCK = 256
SB_TILE = 256
SB_PAIRS_PER_STEP = 4
SB_CUMSUM_PAIRS = 2
ML_CHUNK = 256

LOG2E = 1.4426950408889634
SB_LOG_CUTOFF = -120.0


def _dot(a, b):
    return jnp.dot(a, b, preferred_element_type=F32)


def _dot_nt(a, b):
    return lax.dot_general(a, b, (((1,), (1,)), ((), ())), preferred_element_type=F32)


def _dot_tn(a, b):
    return lax.dot_general(a, b, (((0,), (0,)), ((), ())), preferred_element_type=F32)


def _split_bf16(x):
    hi = x.astype(BF16)
    lo = (x - hi.astype(F32)).astype(BF16)
    return hi, lo


def _sigmoid(x):
    return jax.nn.sigmoid(x)


def _silu(x):
    return x * _sigmoid(x)


def _adaln_kernel(c_ref, w_ref, b_ref, o_ref):
    c = c_ref[...]
    o_ref[0] = _dot(_silu(c).astype(BF16), w_ref[0].astype(BF16)) + b_ref[0]


def _adaln(c, w_ada, b_ada):
    depth, d, _ = w_ada.shape
    bsz = c.shape[0]
    return pl.pallas_call(
        _adaln_kernel,
        grid=(depth, 3),
        in_specs=[
            pl.BlockSpec((bsz, d), lambda l, j: (0, 0)),
            pl.BlockSpec((1, d, d), lambda l, j: (l, 0, j)),
            pl.BlockSpec((1, 1, d), lambda l, j: (l, 0, j)),
        ],
        out_specs=pl.BlockSpec((1, bsz, d), lambda l, j: (l, 0, j)),
        out_shape=jax.ShapeDtypeStruct((depth, bsz, 3 * d), F32),
        compiler_params=pltpu.CompilerParams(
            dimension_semantics=("arbitrary", "arbitrary"), vmem_limit_bytes=VMEM_LIMIT_BYTES),
        name="adaln",
    )(c, w_ada, b_ada.reshape(depth, 1, 3 * d))


def _mlstm_chunk(q, k, v, og, zg, gates, norm_g, c_prev, m_prev, emit):
    cl = q.shape[0]
    hd = ML_HEAD_DIM
    heads = range(ML_HEADS)
    hsl = [slice(h * hd, (h + 1) * hd) for h in heads]
    row = lax.broadcasted_iota(jnp.int32, (cl, cl), 0)
    col = lax.broadcasted_iota(jnp.int32, (cl, cl), 1)
    tril = col <= row
    tril_bf = jnp.where(tril, 1.0, 0.0).astype(BF16)

    def wide(x):
        return jnp.concatenate([x, x], axis=1)

    log_f = jnp.minimum(gates, 0.0) - jnp.log(1.0 + jnp.exp(-jnp.abs(gates)))
    f_hi, f_lo = _split_bf16(log_f)
    b_all = _dot(tril_bf, f_hi) + _dot(tril_bf, f_lo)
    lane = lax.broadcasted_iota(jnp.int32, (cl, LANES), 1)
    rows_t = jnp.where(lane < ML_HEADS, gates, b_all).T
    ones = jnp.ones((cl, hd), BF16)

    qs = [q[:, hsl[h]] for h in heads]
    ks = [k[:, hsl[h]] for h in heads]
    v_aug = [jnp.concatenate([v[:, hsl[h]], ones], axis=1) for h in heads]
    b_rep = [jnp.broadcast_to(b_all[:, ML_HEADS + h:ML_HEADS + h + 1], (cl, LANES)) for h in heads]
    i_rep = [jnp.broadcast_to(gates[:, h:h + 1], (cl, LANES)) for h in heads]

    qk = [_dot_nt(qs[h], ks[h]) for h in heads]
    carried = [_dot(qs[h], c_prev[h].astype(BF16)) for h in heads]

    d_mat, m_t, inter = [], [], []
    for h in heads:
        r_row = rows_t[h:h + 1, :] - rows_t[ML_HEADS + h:ML_HEADS + h + 1, :]
        d = jnp.where(tril, wide(b_rep[h]) + r_row, -jnp.inf)
        it = b_rep[h] + m_prev[h]
        d_mat.append(d)
        inter.append(it)
        m_t.append(jnp.maximum(it, jnp.max(d, axis=-1, keepdims=True)))
    emit()

    s_mat = [(jnp.exp(d_mat[h] - wide(m_t[h])) * qk[h]).astype(BF16) for h in heads]
    intra = [_dot(s_mat[h], v_aug[h]) for h in heads]

    c_new, m_new = [], []
    for h in heads:
        b_last = b_rep[h][cl - 1:cl, :]
        g_rep = b_last - b_rep[h] + i_rep[h]
        m_n = jnp.maximum(b_last + m_prev[h], jnp.max(g_rep, axis=0, keepdims=True))
        decay = jnp.exp(b_last + m_prev[h] - m_n)
        w_k = jnp.exp(g_rep - m_n)
        wv = (wide(w_k) * v_aug[h].astype(F32)).astype(BF16)
        c_new.append(wide(decay) * c_prev[h] + _dot_tn(ks[h], wv))
        m_new.append(m_n)
    emit()

    outs = []
    for h in heads:
        w_inter = jnp.exp(inter[h] - m_t[h])
        num = w_inter * carried[h][:, :hd] + intra[h][:, :hd]
        den = w_inter * carried[h][:, hd:] + intra[h][:, hd:]
        hid = num / jnp.maximum(jnp.abs(den), jnp.exp(-m_t[h]))
        mu = jnp.mean(hid, axis=-1, keepdims=True)
        cen = hid - mu
        var = jnp.mean(cen * cen, axis=-1, keepdims=True)
        normed = cen * lax.rsqrt(var + LN_EPS) * norm_g[:, hsl[h]]
        o_h = og[:, hsl[h]].astype(F32)
        z_h = zg[:, hsl[h]].astype(F32)
        outs.append((normed * _sigmoid(o_h) * _silu(z_h)).astype(BF16))
    return outs, c_new, m_new


def _inproj_kernel(x_ref, xh_ref, mod_ref, w_ref, b_ref, wif_ref, bif_ref, cw_ref, cb_ref, ng_ref,
                   main_ref, uml_ref, c_scr, m_scr, *, tiles_per_seq):
    first = (pl.program_id(0) % tiles_per_seq) == 0

    @pl.when(first)
    def _():
        c_scr[...] = jnp.zeros_like(c_scr)
        m_scr[...] = jnp.zeros_like(m_scr)

    tm = x_ref.shape[0]
    shift = mod_ref[0, 0]
    scale = mod_ref[0, 1]
    h = (x_ref[...] * (1.0 + scale) + shift).astype(BF16)
    h_halo = (xh_ref[...] * (1.0 + scale) + shift).astype(BF16)
    h_ext = jnp.concatenate([h_halo, h], axis=0)

    def column_group(c0):
        cs = slice(c0, c0 + COL_GROUP)
        if ML_Q0 <= c0 < ML_Q0 + 2 * ML_WIDTH:
            ks = slice(c0 - ML_Q0, c0 - ML_Q0 + COL_GROUP)
            ext = _dot(h_ext, w_ref[:, cs]) + b_ref[:, cs]
            acc = ext[HALO:, :]
            halo = jnp.where(first, 0.0, ext[HALO - SUBLANES:HALO, :])
            cat = jnp.concatenate([halo, acc], axis=0)
            y = cb_ref[:, ks] + cw_ref[CONV_WIDTH - 1:CONV_WIDTH, ks] * acc
            for j in range(CONV_WIDTH - 1):
                shifted = pltpu.roll(cat, CONV_WIDTH - 1 - j, axis=0)[SUBLANES:, :]
                y = y + cw_ref[j:j + 1, ks] * shifted
            acc = _silu(y)
            if c0 < ML_Q0 + ML_WIDTH:
                acc = acc * (ML_HEAD_DIM ** -0.5)
            return acc
        acc = _dot(h, w_ref[:, cs]) + b_ref[:, cs]
        if SB0 <= c0 < SB0 + SB_WIDTH:
            acc = acc * (SB_HEAD_DIM ** -0.5)
        return acc

    ml = [column_group(ML_Q0 + g * ML_WIDTH).astype(BF16) for g in range(5)]
    gates = _dot(h, wif_ref[...]) + bif_ref[...]

    pending = list(range(0, main_ref.shape[1], COL_GROUP))
    n_chunks = tm // ML_CHUNK
    per_emit = -(-len(pending) // (2 * n_chunks))

    def emit():
        for _ in range(min(per_emit, len(pending))):
            c0 = pending.pop(0)
            main_ref[:, c0:c0 + COL_GROUP] = column_group(c0).astype(BF16)

    c_state = [c_scr[hh] for hh in range(ML_HEADS)]
    m_state = [m_scr[hh] for hh in range(ML_HEADS)]
    norm_g = ng_ref[...]
    for n in range(n_chunks):
        rs = slice(n * ML_CHUNK, (n + 1) * ML_CHUNK)
        outs, c_state, m_state = _mlstm_chunk(ml[0][rs], ml[1][rs], ml[2][rs], ml[3][rs], ml[4][rs],
                                              gates[rs], norm_g, c_state, m_state, emit)
        for hh in range(ML_HEADS):
            uml_ref[rs, hh * ML_HEAD_DIM:(hh + 1) * ML_HEAD_DIM] = outs[hh]
    while pending:
        emit()
    for hh in range(ML_HEADS):
        c_scr[hh] = c_state[hh]
        m_scr[hh] = m_state[hh]


def _inproj(x2, mod, wm, bm, wif, bif, cw, cb, norm_g, *, seq):
    t, d = x2.shape
    tm = TOKEN_TILE
    n_cols = wm.shape[1]
    tiles_per_seq = seq // tm
    const = lambda i: (0, 0)
    return pl.pallas_call(
        functools.partial(_inproj_kernel, tiles_per_seq=tiles_per_seq),
        grid=(t // tm,),
        in_specs=[
            pl.BlockSpec((tm, d), lambda i: (i, 0)),
            pl.BlockSpec((HALO, d), lambda i: (jnp.maximum(i * (tm // HALO) - 1, 0), 0)),
            pl.BlockSpec((1, 3, 1, d), lambda i: (i // tiles_per_seq, 0, 0, 0)),
            pl.BlockSpec((d, n_cols), const),
            pl.BlockSpec((1, n_cols), const),
            pl.BlockSpec((d, LANES), const),
            pl.BlockSpec((1, LANES), const),
            pl.BlockSpec((CONV_WIDTH, 2 * ML_WIDTH), const),
            pl.BlockSpec((1, 2 * ML_WIDTH), const),
            pl.BlockSpec((1, ML_WIDTH), const),
        ],
        out_specs=[
            pl.BlockSpec((tm, ML_Q0), lambda i: (i, 0)),
            pl.BlockSpec((tm, ML_WIDTH), lambda i: (i, 0)),
        ],
        out_shape=[
            jax.ShapeDtypeStruct((t, ML_Q0), BF16),
            jax.ShapeDtypeStruct((t, ML_WIDTH), BF16),
        ],
        scratch_shapes=[pltpu.VMEM((ML_HEADS, ML_HEAD_DIM, 2 * ML_HEAD_DIM), F32),
                        pltpu.VMEM((ML_HEADS, 1, LANES), F32)],
        compiler_params=pltpu.CompilerParams(
            dimension_semantics=("arbitrary",), vmem_limit_bytes=VMEM_LIMIT_BYTES),
        name="inproj_mlstm",
    )(x2, x2, mod, wm, bm, wif, bif, cw, cb, norm_g)


def _sb_kernel(q_ref, k_ref, v_ref, z_ref, o_ref, *, seq):
    ts = SB_TILE
    n_pairs = q_ref.shape[1] // LANES
    row = lax.broadcasted_iota(jnp.int32, (2 * ts, ts), 0)
    col = lax.broadcasted_iota(jnp.int32, (2 * ts, ts), 1)
    strict = col < (row & (ts - 1))
    upper = jnp.where(lax.broadcasted_iota(jnp.int32, (ts, ts), 0) > lax.broadcasted_iota(jnp.int32, (ts, ts), 1),
                      1.0, 0.0).astype(BF16)
    upper2 = jnp.concatenate([upper, upper], axis=0)
    head0 = lax.broadcasted_iota(jnp.int32, (ts, LANES), 1) < SB_HEAD_DIM

    psl = [slice(p * LANES, (p + 1) * LANES) for p in range(n_pairs)]

    def load_queries(q0):
        qs = []
        for p in range(n_pairs):
            q = q_ref[pl.ds(q0, ts), psl[p]]
            zero = jnp.zeros_like(q)
            qs.append(jnp.concatenate([jnp.where(head0, q, zero), jnp.where(head0, zero, q)], axis=0))
        return qs

    def key_tiles(entries, qs, carries, accs):
        log_betas, first_cols, suffixes = [], [], []
        group = SB_CUMSUM_PAIRS
        for a, j0, diag in entries:
            log_rems = []
            for p in range(n_pairs):
                z = _dot_nt(qs[a][p], k_ref[pl.ds(j0, ts), psl[p]])
                log_beta = jnp.minimum(z, 0.0) - jnp.log(1.0 + jnp.exp2(jnp.abs(z) * -LOG2E))
                log_rem = log_beta - z
                if diag:
                    log_rem = jnp.where(strict, log_rem, 0.0)
                log_betas.append(log_beta)
                log_rems.append(log_rem)
                first_cols.append(log_rem[:, 0:1])
                if len(log_rems) == group:
                    hi, lo = _split_bf16(jnp.concatenate(log_rems, axis=0))
                    suffixes.append(_dot(jnp.concatenate([hi, lo], axis=1), upper2))
                    log_rems = []
        carries, accs = list(carries), list(accs)
        for n, (a, j0, diag) in enumerate(entries):
            new_c, new_a = [], []
            for p in range(n_pairs):
                m = n * n_pairs + p
                suffix = suffixes[m // group][2 * ts * (m % group):2 * ts * (m % group + 1)]
                attn = jnp.exp((log_betas[m] + suffix) + carries[a][p])
                if diag:
                    attn = jnp.where(strict, attn, 0.0)
                new_a.append(accs[a][p] + _dot(attn.astype(BF16), v_ref[pl.ds(j0, ts), psl[p]]))
                new_c.append(carries[a][p] + (suffix[:, 0:1] + first_cols[m]))
            carries[a], accs[a] = tuple(new_c), tuple(new_a)
        return carries, accs

    def alive(carries):
        top = carries[0]
        for c in carries[1:]:
            top = jnp.maximum(top, c)
        return jnp.max(top) > SB_LOG_CUTOFF

    def remaining_tiles(i, qs, carries, accs):
        def cond(st):
            jj, go, _, _ = st
            return jnp.logical_and(jj < i, go)

        def body(st):
            jj, _, carries, accs = st
            j0 = pl.multiple_of((i - 1 - jj) * ts, ts)
            (carries,), (accs,) = key_tiles([(0, j0, False)], [qs], [carries], [accs])
            return jj + 1, alive(carries), carries, accs

        _, _, carries, accs = lax.while_loop(cond, body, (jnp.int32(1), alive(carries), carries, accs))
        return carries, accs

    def q_tile(i, has_prev):
        if has_prev:
            q0 = pl.multiple_of(i * ts, ts)
            entries = [(0, q0, True), (0, pl.multiple_of(q0 - ts, ts), False)]
        else:
            q0 = 0
            entries = [(0, q0, True)]
        qs = load_queries(q0)
        zero_c = tuple(jnp.zeros((2 * ts, 1), F32) for _ in range(n_pairs))
        zero_a = tuple(jnp.zeros((2 * ts, LANES), F32) for _ in range(n_pairs))
        (carries,), (accs,) = key_tiles(entries, [qs], [zero_c], [zero_a])
        if has_prev:
            carries, accs = remaining_tiles(i, qs, carries, accs)
        for p in range(n_pairs):
            acc = jnp.where(head0, accs[p][:ts], accs[p][ts:])
            o_ref[pl.ds(q0, ts), psl[p]] = (acc * _silu(z_ref[pl.ds(q0, ts), psl[p]].astype(F32))).astype(BF16)

    q_tile(0, False)

    def q_body(i, carry):
        q_tile(i, True)
        return carry

    lax.fori_loop(1, seq // ts, q_body, 0)


def _sb_attention(main, *, batch, seq):
    t = main.shape[0]
    width = SB_PAIRS_PER_STEP * LANES
    n_steps = SB_WIDTH // width
    first = SB0 // width
    blk = lambda g: pl.BlockSpec((seq, width), lambda b, p: (b, first + g * n_steps + p))
    return pl.pallas_call(
        functools.partial(_sb_kernel, seq=seq),
        grid=(batch, n_steps),
        in_specs=[blk(0), blk(1), blk(2), blk(3)],
        out_specs=pl.BlockSpec((seq, width), lambda b, p: (b, p)),
        out_shape=jax.ShapeDtypeStruct((t, SB_WIDTH), BF16),
        compiler_params=pltpu.CompilerParams(
            dimension_semantics=("arbitrary", "arbitrary"), vmem_limit_bytes=VMEM_LIMIT_BYTES),
        name="sb_attention",
    )(main, main, main, main)


def _out_kernel(usb_ref, uml_ref, gate_ref, x_ref, mod_ref, wsb_ref, wml_ref, wo_ref,
                lng_ref, lnb_ref, o_ref, *, alpha):
    tm, d = x_ref.shape
    gate = mod_ref[0, 2]
    blocks = [slice(r0, r0 + OUT_ROW_BLOCK) for r0 in range(0, tm, OUT_ROW_BLOCK)]
    y_sb = [_dot(usb_ref[rs, :], wsb_ref[...]) for rs in blocks]
    y_ml = [_dot(uml_ref[rs, :], wml_ref[...]) for rs in blocks]
    mix = [(_sigmoid(gate_ref[rs, :d].astype(F32)) * y_sb[n]
            + _sigmoid(gate_ref[rs, d:].astype(F32)) * y_ml[n]).astype(BF16) for n, rs in enumerate(blocks)]
    y = [_dot(mix[n], wo_ref[...]) for n in range(len(blocks))]
    for n, rs in enumerate(blocks):
        r = alpha * x_ref[rs, :] + (1.0 + gate) * y[n]
        mu = jnp.mean(r, axis=-1, keepdims=True)
        cen = r - mu
        var = jnp.mean(cen * cen, axis=-1, keepdims=True)
        o_ref[rs, :] = cen * lax.rsqrt(var + LN_EPS) * lng_ref[...] + lnb_ref[...]


def _out_proj(usb, uml, main, x2, mod, wsb, wml, wo, lng, lnb, *, seq, alpha):
    t, d = x2.shape
    tm = TOKEN_TILE
    tiles_per_seq = seq // tm
    const = lambda i: (0, 0)
    return pl.pallas_call(
        functools.partial(_out_kernel, alpha=alpha),
        grid=(t // tm,),
        in_specs=[
            pl.BlockSpec((tm, SB_WIDTH), lambda i: (i, 0)),
            pl.BlockSpec((tm, ML_WIDTH), lambda i: (i, 0)),
            pl.BlockSpec((tm, 2 * d), lambda i: (i, 0)),
            pl.BlockSpec((tm, d), lambda i: (i, 0)),
            pl.BlockSpec((1, 3, 1, d), lambda i: (i // tiles_per_seq, 0, 0, 0)),
            pl.BlockSpec((SB_WIDTH, d), const),
            pl.BlockSpec((ML_WIDTH, d), const),
            pl.BlockSpec((d, d), const),
            pl.BlockSpec((1, d), const),
            pl.BlockSpec((1, d), const),
        ],
        out_specs=pl.BlockSpec((tm, d), lambda i: (i, 0)),
        out_shape=jax.ShapeDtypeStruct((t, d), F32),
        compiler_params=pltpu.CompilerParams(
            dimension_semantics=("arbitrary",), vmem_limit_bytes=VMEM_LIMIT_BYTES),
        name="out_proj",
    )(usb, uml, main, x2, mod, wsb, wml, wo, lng, lnb)


def kernel(x, c, w_ada, b_ada, w_in, b_in, conv_w, conv_b, ml_norm_g, w_sb_proj, w_ml_proj, w_out, ln_g, ln_b):
    bsz, seq, d = x.shape
    depth = w_ada.shape[0]
    assert d == D_MODEL
    assert seq % TOKEN_TILE == 0 and seq % SB_TILE == 0 and TOKEN_TILE % ML_CHUNK == 0
    alpha = (2.0 * depth) ** 0.25
    if0 = 4 * SB_WIDTH + 5 * ML_WIDTH
    g_start = if0 + 2 * ML_HEADS

    mod_all = _adaln(c, w_ada, b_ada).reshape(depth, bsz, 3, 1, d)
    x2 = x.reshape(bsz * seq, d)
    for l in range(depth):
        wm = jnp.concatenate([w_in[l, :, g_start:], w_in[l, :, :if0]], axis=1).astype(BF16)
        bm = jnp.concatenate([b_in[l, g_start:], b_in[l, :if0]])[None, :]
        pad = LANES - 2 * ML_HEADS
        wif = jnp.pad(w_in[l, :, if0:g_start], ((0, 0), (0, pad))).astype(BF16)
        bif = jnp.pad(b_in[l, if0:g_start], (0, pad))[None, :]
        main, uml = _inproj(x2, mod_all[l], wm, bm, wif, bif, conv_w[l], conv_b[l][None, :],
                            ml_norm_g[l][None, :], seq=seq)
        usb = _sb_attention(main, batch=bsz, seq=seq)
        x2 = _out_proj(usb, uml, main, x2, mod_all[l],
                       w_sb_proj[l].astype(BF16), w_ml_proj[l].astype(BF16), w_out[l].astype(BF16),
                       ln_g[l][None, :], ln_b[l][None, :], seq=seq, alpha=alpha)
    return x2.reshape(bsz, seq, d)
```

```python
import functools

import jax
import jax.numpy as jnp
from jax import lax
from jax.experimental import pallas as pl
from jax.experimental.pallas import tpu as pltpu

F32 = jnp.float32
BF16 = jnp.bfloat16

SB_HEADS = 8
SB_HEAD_DIM = 64
SB_WIDTH = SB_HEADS * SB_HEAD_DIM
ML_HEADS = 4
ML_HEAD_DIM = 128
ML_WIDTH = ML_HEADS * ML_HEAD_DIM
CONV_WIDTH = 4
LN_EPS = 1e-5

LANES = 128
SUBLANES = 8
VMEM_LIMIT_BYTES = 56 * 1024 * 1024

COL_GROUP = 512
D_MODEL = 1024
SB0 = 2 * D_MODEL
ML_Q0 = SB0 + 4 * SB_WIDTH

TOKEN_TILE = 512
INPROJ_TILE = 512
HALO = 16
OUT_ROW_BLOCK = 256
SB_TILE = 256
SB_PAIRS_PER_STEP = 4
SB_CUMSUM_PAIRS = 2
SB_STAGE_LAG = 2
ML_CHUNK = 256
MLSTM_EMIT_POINTS = 4

LOG2E = 1.4426950408889634
SB_LOG_CUTOFF = -120.0


def _dot(a, b):
    return jnp.dot(a, b, preferred_element_type=F32)


def _dot_nt(a, b):
    return lax.dot_general(a, b, (((1,), (1,)), ((), ())), preferred_element_type=F32)


def _dot_tn(a, b):
    return lax.dot_general(a, b, (((0,), (0,)), ((), ())), preferred_element_type=F32)


def _split_bf16(x):
    hi = x.astype(BF16)
    lo = (x - hi.astype(F32)).astype(BF16)
    return hi, lo


def _sigmoid(x):
    return jax.nn.sigmoid(x)


def _silu(x):
    return x * _sigmoid(x)


def _adaln_kernel(c_ref, w_ref, b_ref, o_ref):
    c = c_ref[...]
    o_ref[0] = _dot(_silu(c).astype(BF16), w_ref[0].astype(BF16)) + b_ref[0]


def _adaln(c, w_ada, b_ada):
    depth, d, _ = w_ada.shape
    bsz = c.shape[0]
    return pl.pallas_call(
        _adaln_kernel,
        grid=(depth, 3),
        in_specs=[
            pl.BlockSpec((bsz, d), lambda l, j: (0, 0)),
            pl.BlockSpec((1, d, d), lambda l, j: (l, 0, j)),
            pl.BlockSpec((1, 1, d), lambda l, j: (l, 0, j)),
        ],
        out_specs=pl.BlockSpec((1, bsz, d), lambda l, j: (l, 0, j)),
        out_shape=jax.ShapeDtypeStruct((depth, bsz, 3 * d), F32),
        compiler_params=pltpu.CompilerParams(
            dimension_semantics=("arbitrary", "arbitrary"), vmem_limit_bytes=VMEM_LIMIT_BYTES),
        name="adaln",
    )(c, w_ada, b_ada.reshape(depth, 1, 3 * d))


def _mlstm_chunk(q, k, v, og, zg, gates, norm_g, c_prev, m_prev, emit):
    cl = q.shape[0]
    hd = ML_HEAD_DIM
    heads = range(ML_HEADS)
    hsl = [slice(h * hd, (h + 1) * hd) for h in heads]
    row = lax.broadcasted_iota(jnp.int32, (cl, cl), 0)
    col = lax.broadcasted_iota(jnp.int32, (cl, cl), 1)
    tril = col <= row
    tril_bf = jnp.where(tril, 1.0, 0.0).astype(BF16)

    def wide(x):
        return jnp.concatenate([x, x], axis=1)

    log_f = jnp.minimum(gates, 0.0) - jnp.log(1.0 + jnp.exp(-jnp.abs(gates)))
    f_hi, f_lo = _split_bf16(log_f)
    b_all = _dot(tril_bf, f_hi) + _dot(tril_bf, f_lo)
    lane = lax.broadcasted_iota(jnp.int32, (cl, LANES), 1)
    rows_t = jnp.where(lane < ML_HEADS, gates, b_all).T
    ones = jnp.ones((cl, hd), BF16)

    qs = [q[:, hsl[h]] for h in heads]
    ks = [k[:, hsl[h]] for h in heads]
    v_aug = [jnp.concatenate([v[:, hsl[h]], ones], axis=1) for h in heads]
    b_rep = [jnp.broadcast_to(b_all[:, ML_HEADS + h:ML_HEADS + h + 1], (cl, LANES)) for h in heads]
    i_rep = [jnp.broadcast_to(gates[:, h:h + 1], (cl, LANES)) for h in heads]

    qk = [_dot_nt(qs[h], ks[h]) for h in heads]
    carried = [_dot(qs[h], c_prev[h].astype(BF16)) for h in heads]
    emit()

    d_mat, m_t, inter = [], [], []
    for h in heads:
        r_row = rows_t[h:h + 1, :] - rows_t[ML_HEADS + h:ML_HEADS + h + 1, :]
        d = jnp.where(tril, wide(b_rep[h]) + r_row, -jnp.inf)
        it = b_rep[h] + m_prev[h]
        d_mat.append(d)
        inter.append(it)
        m_t.append(jnp.maximum(it, jnp.max(d, axis=-1, keepdims=True)))
    emit()

    s_mat = [(jnp.exp(d_mat[h] - wide(m_t[h])) * qk[h]).astype(BF16) for h in heads]
    intra = [_dot(s_mat[h], v_aug[h]) for h in heads]
    emit()

    c_new, m_new = [], []
    for h in heads:
        b_last = b_rep[h][cl - 1:cl, :]
        g_rep = b_last - b_rep[h] + i_rep[h]
        m_n = jnp.maximum(b_last + m_prev[h], jnp.max(g_rep, axis=0, keepdims=True))
        decay = jnp.exp(b_last + m_prev[h] - m_n)
        w_k = jnp.exp(g_rep - m_n)
        wv = (wide(w_k) * v_aug[h].astype(F32)).astype(BF16)
        c_new.append(wide(decay) * c_prev[h] + _dot_tn(ks[h], wv))
        m_new.append(m_n)
    emit()

    outs = []
    for h in heads:
        w_inter = jnp.exp(inter[h] - m_t[h])
        num = w_inter * carried[h][:, :hd] + intra[h][:, :hd]
        den = w_inter * carried[h][:, hd:] + intra[h][:, hd:]
        hid = num / jnp.maximum(jnp.abs(den), jnp.exp(-m_t[h]))
        mu = jnp.mean(hid, axis=-1, keepdims=True)
        cen = hid - mu
        var = jnp.mean(cen * cen, axis=-1, keepdims=True)
        normed = cen * lax.rsqrt(var + LN_EPS) * norm_g[:, hsl[h]]
        o_h = og[:, hsl[h]].astype(F32)
        z_h = zg[:, hsl[h]].astype(F32)
        outs.append((normed * _sigmoid(o_h) * _silu(z_h)).astype(BF16))
    return outs, c_new, m_new


def _inproj_kernel(x_ref, xh_ref, mod_ref, w_ref, b_ref, wif_ref, bif_ref, cw_ref, cb_ref, ng_ref,
                   main_ref, uml_ref, c_scr, m_scr, *, tiles_per_seq):
    first = (pl.program_id(0) % tiles_per_seq) == 0

    @pl.when(first)
    def _():
        c_scr[...] = jnp.zeros_like(c_scr)
        m_scr[...] = jnp.zeros_like(m_scr)

    tm = x_ref.shape[0]
    shift = mod_ref[0, 0]
    scale = mod_ref[0, 1]
    h = (x_ref[...] * (1.0 + scale) + shift).astype(BF16)
    h_halo = (xh_ref[...] * (1.0 + scale) + shift).astype(BF16)
    h_ext = jnp.concatenate([h_halo, h], axis=0)

    def column_group(c0):
        cs = slice(c0, c0 + COL_GROUP)
        if ML_Q0 <= c0 < ML_Q0 + 2 * ML_WIDTH:
            ks = slice(c0 - ML_Q0, c0 - ML_Q0 + COL_GROUP)
            ext = _dot(h_ext, w_ref[:, cs]) + b_ref[:, cs]
            acc = ext[HALO:, :]
            halo = jnp.where(first, 0.0, ext[HALO - SUBLANES:HALO, :])
            cat = jnp.concatenate([halo, acc], axis=0)
            y = cb_ref[:, ks] + cw_ref[CONV_WIDTH - 1:CONV_WIDTH, ks] * acc
            for j in range(CONV_WIDTH - 1):
                shifted = pltpu.roll(cat, CONV_WIDTH - 1 - j, axis=0)[SUBLANES:, :]
                y = y + cw_ref[j:j + 1, ks] * shifted
            acc = _silu(y)
            if c0 < ML_Q0 + ML_WIDTH:
                acc = acc * (ML_HEAD_DIM ** -0.5)
            return acc
        acc = _dot(h, w_ref[:, cs]) + b_ref[:, cs]
        if SB0 <= c0 < SB0 + SB_WIDTH:
            acc = acc * (SB_HEAD_DIM ** -0.5)
        return acc

    ml = [column_group(ML_Q0 + g * ML_WIDTH).astype(BF16) for g in range(5)]
    gates = _dot(h, wif_ref[...]) + bif_ref[...]

    pending = list(range(0, main_ref.shape[1], COL_GROUP))
    n_chunks = tm // ML_CHUNK
    per_emit = -(-len(pending) // (MLSTM_EMIT_POINTS * n_chunks))

    def emit():
        for _ in range(min(per_emit, len(pending))):
            c0 = pending.pop(0)
            main_ref[:, c0:c0 + COL_GROUP] = column_group(c0).astype(BF16)

    c_state = [c_scr[hh] for hh in range(ML_HEADS)]
    m_state = [m_scr[hh] for hh in range(ML_HEADS)]
    norm_g = ng_ref[...]
    for n in range(n_chunks):
        rs = slice(n * ML_CHUNK, (n + 1) * ML_CHUNK)
        outs, c_state, m_state = _mlstm_chunk(ml[0][rs], ml[1][rs], ml[2][rs], ml[3][rs], ml[4][rs],
                                              gates[rs], norm_g, c_state, m_state, emit)
        for hh in range(ML_HEADS):
            uml_ref[rs, hh * ML_HEAD_DIM:(hh + 1) * ML_HEAD_DIM] = outs[hh]
    while pending:
        emit()
    for hh in range(ML_HEADS):
        c_scr[hh] = c_state[hh]
        m_scr[hh] = m_state[hh]


def _inproj(x2, mod, wm, bm, wif, bif, cw, cb, norm_g, *, seq):
    t, d = x2.shape
    tm = INPROJ_TILE
    n_cols = wm.shape[1]
    tiles_per_seq = seq // tm
    const = lambda i: (0, 0)
    return pl.pallas_call(
        functools.partial(_inproj_kernel, tiles_per_seq=tiles_per_seq),
        grid=(t // tm,),
        in_specs=[
            pl.BlockSpec((tm, d), lambda i: (i, 0)),
            pl.BlockSpec((HALO, d), lambda i: (jnp.maximum(i * (tm // HALO) - 1, 0), 0)),
            pl.BlockSpec((1, 3, 1, d), lambda i: (i // tiles_per_seq, 0, 0, 0)),
            pl.BlockSpec((d, n_cols), const),
            pl.BlockSpec((1, n_cols), const),
            pl.BlockSpec((d, LANES), const),
            pl.BlockSpec((1, LANES), const),
            pl.BlockSpec((CONV_WIDTH, 2 * ML_WIDTH), const),
            pl.BlockSpec((1, 2 * ML_WIDTH), const),
            pl.BlockSpec((1, ML_WIDTH), const),
        ],
        out_specs=[
            pl.BlockSpec((tm, ML_Q0), lambda i: (i, 0)),
            pl.BlockSpec((tm, ML_WIDTH), lambda i: (i, 0)),
        ],
        out_shape=[
            jax.ShapeDtypeStruct((t, ML_Q0), BF16),
            jax.ShapeDtypeStruct((t, ML_WIDTH), BF16),
        ],
        scratch_shapes=[pltpu.VMEM((ML_HEADS, ML_HEAD_DIM, 2 * ML_HEAD_DIM), F32),
                        pltpu.VMEM((ML_HEADS, 1, LANES), F32)],
        compiler_params=pltpu.CompilerParams(
            dimension_semantics=("arbitrary",), vmem_limit_bytes=VMEM_LIMIT_BYTES),
        name="inproj_mlstm",
    )(x2, x2, mod, wm, bm, wif, bif, cw, cb, norm_g)


def _sb_kernel(q_ref, k_ref, v_ref, z_ref, o_ref, *, seq):
    ts = SB_TILE
    n_pairs = q_ref.shape[1] // LANES
    row = lax.broadcasted_iota(jnp.int32, (2 * ts, ts), 0)
    col = lax.broadcasted_iota(jnp.int32, (2 * ts, ts), 1)
    strict = col < (row & (ts - 1))
    upper = jnp.where(lax.broadcasted_iota(jnp.int32, (ts, ts), 0) > lax.broadcasted_iota(jnp.int32, (ts, ts), 1),
                      1.0, 0.0).astype(BF16)
    upper2 = jnp.concatenate([upper, upper], axis=0)
    head0 = lax.broadcasted_iota(jnp.int32, (ts, LANES), 1) < SB_HEAD_DIM

    psl = [slice(p * LANES, (p + 1) * LANES) for p in range(n_pairs)]

    def load_queries(q0):
        qs = []
        for p in range(n_pairs):
            q = q_ref[pl.ds(q0, ts), psl[p]]
            zero = jnp.zeros_like(q)
            qs.append(jnp.concatenate([jnp.where(head0, q, zero), jnp.where(head0, zero, q)], axis=0))
        return qs

    def key_tiles(entries, qs, carries, accs):
        group = SB_CUMSUM_PAIRS
        units = [(n, g) for n in range(len(entries)) for g in range(n_pairs // group)]
        log_betas, first_cols, suffixes = {}, {}, {}
        carries = [list(c) for c in carries]
        accs = [list(a) for a in accs]

        def stage1(n, g):
            a, j0, diag = entries[n]
            log_rems = []
            for p in range(g * group, (g + 1) * group):
                z = _dot_nt(qs[a][p], k_ref[pl.ds(j0, ts), psl[p]])
                log_beta = jnp.minimum(z, 0.0) - jnp.log(1.0 + jnp.exp2(jnp.abs(z) * -LOG2E))
                log_rem = log_beta - z
                if diag:
                    log_rem = jnp.where(strict, log_rem, 0.0)
                log_betas[n, p] = log_beta
                first_cols[n, p] = log_rem[:, 0:1]
                log_rems.append(log_rem)
            hi, lo = _split_bf16(jnp.concatenate(log_rems, axis=0))
            suffixes[n, g] = _dot(jnp.concatenate([hi, lo], axis=1), upper2)

        def stage2(n, g):
            a, j0, diag = entries[n]
            for r, p in enumerate(range(g * group, (g + 1) * group)):
                suffix = suffixes[n, g][2 * ts * r:2 * ts * (r + 1)]
                attn = jnp.exp((log_betas[n, p] + suffix) + carries[a][p])
                if diag:
                    attn = jnp.where(strict, attn, 0.0)
                accs[a][p] = accs[a][p] + _dot(attn.astype(BF16), v_ref[pl.ds(j0, ts), psl[p]])
                carries[a][p] = carries[a][p] + (suffix[:, 0:1] + first_cols[n, p])

        for step in range(len(units) + SB_STAGE_LAG):
            if step < len(units):
                stage1(*units[step])
            if step >= SB_STAGE_LAG:
                stage2(*units[step - SB_STAGE_LAG])
        return [tuple(c) for c in carries], [tuple(a) for a in accs]

    def alive(carries):
        top = carries[0]
        for c in carries[1:]:
            top = jnp.maximum(top, c)
        return jnp.max(top) > SB_LOG_CUTOFF

    def remaining_tiles(i, qs, carries, accs):
        def cond(st):
            jj, go, _, _ = st
            return jnp.logical_and(jj < i, go)

        def body(st):
            jj, _, carries, accs = st
            j0 = pl.multiple_of((i - 1 - jj) * ts, ts)
            (carries,), (accs,) = key_tiles([(0, j0, False)], [qs], [carries], [accs])
            return jj + 1, alive(carries), carries, accs

        _, _, carries, accs = lax.while_loop(cond, body, (jnp.int32(1), alive(carries), carries, accs))
        return carries, accs

    def q_tile_pair(m, first):
        ia = 2 * m
        if first:
            q0a, q0b = 0, ts
            entries = [(0, q0a, True), (1, q0b, True), (1, q0a, False)]
        else:
            q0a = pl.multiple_of(ia * ts, ts)
            q0b = pl.multiple_of(q0a + ts, ts)
            entries = [(0, q0a, True), (1, q0b, True), (0, pl.multiple_of(q0a - ts, ts), False), (1, q0a, False)]
        qs = [load_queries(q0a), load_queries(q0b)]
        zero_c = tuple(jnp.zeros((2 * ts, 1), F32) for _ in range(n_pairs))
        zero_a = tuple(jnp.zeros((2 * ts, LANES), F32) for _ in range(n_pairs))
        carries, accs = key_tiles(entries, qs, [zero_c, zero_c], [zero_a, zero_a])
        if not first:
            carries[0], accs[0] = remaining_tiles(ia, qs[0], carries[0], accs[0])
            carries[1], accs[1] = remaining_tiles(ia + 1, qs[1], carries[1], accs[1])
        for a, q0 in enumerate((q0a, q0b)):
            for p in range(n_pairs):
                acc = jnp.where(head0, accs[a][p][:ts], accs[a][p][ts:])
                gate = _silu(z_ref[pl.ds(q0, ts), psl[p]].astype(F32))
                o_ref[pl.ds(q0, ts), psl[p]] = (acc * gate).astype(BF16)

    q_tile_pair(0, True)

    def pair_body(m, carry):
        q_tile_pair(m, False)
        return carry

    lax.fori_loop(1, seq // (2 * ts), pair_body, 0)


def _sb_attention(main, *, batch, seq):
    t = main.shape[0]
    width = SB_PAIRS_PER_STEP * LANES
    n_steps = SB_WIDTH // width
    first = SB0 // width
    blk = lambda g: pl.BlockSpec((seq, width), lambda b, p: (b, first + g * n_steps + p))
    return pl.pallas_call(
        functools.partial(_sb_kernel, seq=seq),
        grid=(batch, n_steps),
        in_specs=[blk(0), blk(1), blk(2), blk(3)],
        out_specs=pl.BlockSpec((seq, width), lambda b, p: (b, p)),
        out_shape=jax.ShapeDtypeStruct((t, SB_WIDTH), BF16),
        compiler_params=pltpu.CompilerParams(
            dimension_semantics=("arbitrary", "arbitrary"), vmem_limit_bytes=VMEM_LIMIT_BYTES),
        name="sb_attention",
    )(main, main, main, main)


def _out_kernel(usb_ref, uml_ref, gate_ref, x_ref, mod_ref, wsb_ref, wml_ref, wo_ref,
                lng_ref, lnb_ref, o_ref, *, alpha):
    tm, d = x_ref.shape
    gate = mod_ref[0, 2]
    blocks = [slice(r0, r0 + OUT_ROW_BLOCK) for r0 in range(0, tm, OUT_ROW_BLOCK)]
    y_sb = [_dot(usb_ref[rs, :], wsb_ref[...]) for rs in blocks]
    y_ml = [_dot(uml_ref[rs, :], wml_ref[...]) for rs in blocks]
    mix = [(_sigmoid(gate_ref[rs, :d].astype(F32)) * y_sb[n]
            + _sigmoid(gate_ref[rs, d:].astype(F32)) * y_ml[n]).astype(BF16) for n, rs in enumerate(blocks)]
    y = [_dot(mix[n], wo_ref[...]) for n in range(len(blocks))]
    for n, rs in enumerate(blocks):
        r = alpha * x_ref[rs, :] + (1.0 + gate) * y[n]
        mu = jnp.mean(r, axis=-1, keepdims=True)
        cen = r - mu
        var = jnp.mean(cen * cen, axis=-1, keepdims=True)
        o_ref[rs, :] = cen * lax.rsqrt(var + LN_EPS) * lng_ref[...] + lnb_ref[...]


def _out_proj(usb, uml, main, x2, mod, wsb, wml, wo, lng, lnb, *, seq, alpha):
    t, d = x2.shape
    tm = TOKEN_TILE
    tiles_per_seq = seq // tm
    const = lambda i: (0, 0)
    return pl.pallas_call(
        functools.partial(_out_kernel, alpha=alpha),
        grid=(t // tm,),
        in_specs=[
            pl.BlockSpec((tm, SB_WIDTH), lambda i: (i, 0)),
            pl.BlockSpec((tm, ML_WIDTH), lambda i: (i, 0)),
            pl.BlockSpec((tm, 2 * d), lambda i: (i, 0)),
            pl.BlockSpec((tm, d), lambda i: (i, 0)),
            pl.BlockSpec((1, 3, 1, d), lambda i: (i // tiles_per_seq, 0, 0, 0)),
            pl.BlockSpec((SB_WIDTH, d), const),
            pl.BlockSpec((ML_WIDTH, d), const),
            pl.BlockSpec((d, d), const),
            pl.BlockSpec((1, d), const),
            pl.BlockSpec((1, d), const),
        ],
        out_specs=pl.BlockSpec((tm, d), lambda i: (i, 0)),
        out_shape=jax.ShapeDtypeStruct((t, d), F32),
        compiler_params=pltpu.CompilerParams(
            dimension_semantics=("arbitrary",), vmem_limit_bytes=VMEM_LIMIT_BYTES),
        name="out_proj",
    )(usb, uml, main, x2, mod, wsb, wml, wo, lng, lnb)


def kernel(x, c, w_ada, b_ada, w_in, b_in, conv_w, conv_b, ml_norm_g, w_sb_proj, w_ml_proj, w_out, ln_g, ln_b):
    bsz, seq, d = x.shape
    depth = w_ada.shape[0]
    assert d == D_MODEL
    assert seq % TOKEN_TILE == 0 and seq % INPROJ_TILE == 0 and INPROJ_TILE % ML_CHUNK == 0
    assert seq % (2 * SB_TILE) == 0
    alpha = (2.0 * depth) ** 0.25
    if0 = 4 * SB_WIDTH + 5 * ML_WIDTH
    g_start = if0 + 2 * ML_HEADS

    mod_all = _adaln(c, w_ada, b_ada).reshape(depth, bsz, 3, 1, d)
    x2 = x.reshape(bsz * seq, d)
    for l in range(depth):
        wm = jnp.concatenate([w_in[l, :, g_start:], w_in[l, :, :if0]], axis=1).astype(BF16)
        bm = jnp.concatenate([b_in[l, g_start:], b_in[l, :if0]])[None, :]
        pad = LANES - 2 * ML_HEADS
        wif = jnp.pad(w_in[l, :, if0:g_start], ((0, 0), (0, pad))).astype(BF16)
        bif = jnp.pad(b_in[l, if0:g_start], (0, pad))[None, :]
        main, uml = _inproj(x2, mod_all[l], wm, bm, wif, bif, conv_w[l], conv_b[l][None, :],
                            ml_norm_g[l][None, :], seq=seq)
        usb = _sb_attention(main, batch=bsz, seq=seq)
        x2 = _out_proj(usb, uml, main, x2, mod_all[l],
                       w_sb_proj[l].astype(BF16), w_ml_proj[l].astype(BF16), w_out[l].astype(BF16),
                       ln_g[l][None, :], ln_b[l][None, :], seq=seq, alpha=alpha)
    return x2.reshape(bsz, seq, d)
```

```python
import functools

import jax
import jax.numpy as jnp
from jax import lax
from jax.experimental import pallas as pl
from jax.experimental.pallas import tpu as pltpu

F32 = jnp.float32
BF16 = jnp.bfloat16

SB_HEADS = 8
SB_HEAD_DIM = 64
SB_WIDTH = SB_HEADS * SB_HEAD_DIM
ML_HEADS = 4
ML_HEAD_DIM = 128
ML_WIDTH = ML_HEADS * ML_HEAD_DIM
CONV_WIDTH = 4
LN_EPS = 1e-5

LANES = 128
SUBLANES = 8
VMEM_LIMIT_BYTES = 56 * 1024 * 1024

COL_GROUP = 512
D_MODEL = 1024
SB0 = 2 * D_MODEL
ML_Q0 = SB0 + 4 * SB_WIDTH

TOKEN_TILE = 512
INPROJ_TILE = 512
HALO = 16
OUT_ROW_BLOCK = 256
SB_TILE = 256
SB_PAIRS_PER_STEP = 4
SB_CUMSUM_PAIRS = 2
SB_STAGE_LAG = 2
ML_CHUNK = 256
MLSTM_EMIT_POINTS = 4

LOG2E = 1.4426950408889634
SB_LOG_CUTOFF = -120.0


def _dot(a, b):
    return jnp.dot(a, b, preferred_element_type=F32)


def _dot_nt(a, b):
    return lax.dot_general(a, b, (((1,), (1,)), ((), ())), preferred_element_type=F32)


def _dot_tn(a, b):
    return lax.dot_general(a, b, (((0,), (0,)), ((), ())), preferred_element_type=F32)


def _split_bf16(x):
    hi = x.astype(BF16)
    lo = (x - hi.astype(F32)).astype(BF16)
    return hi, lo


def _sigmoid(x):
    return jax.nn.sigmoid(x)


def _silu(x):
    return x * _sigmoid(x)


def _adaln_kernel(c_ref, w_ref, b_ref, o_ref):
    c = c_ref[...]
    o_ref[0] = _dot(_silu(c).astype(BF16), w_ref[0].astype(BF16)) + b_ref[0]


def _adaln(c, w_ada, b_ada):
    depth, d, _ = w_ada.shape
    bsz = c.shape[0]
    return pl.pallas_call(
        _adaln_kernel,
        grid=(depth, 3),
        in_specs=[
            pl.BlockSpec((bsz, d), lambda l, j: (0, 0)),
            pl.BlockSpec((1, d, d), lambda l, j: (l, 0, j)),
            pl.BlockSpec((1, 1, d), lambda l, j: (l, 0, j)),
        ],
        out_specs=pl.BlockSpec((1, bsz, d), lambda l, j: (l, 0, j)),
        out_shape=jax.ShapeDtypeStruct((depth, bsz, 3 * d), F32),
        compiler_params=pltpu.CompilerParams(
            dimension_semantics=("arbitrary", "arbitrary"), vmem_limit_bytes=VMEM_LIMIT_BYTES),
        name="adaln",
    )(c, w_ada, b_ada.reshape(depth, 1, 3 * d))


def _mlstm_chunk(q, k, v, og, zg, gates, norm_g, c_prev, m_prev, emit):
    cl = q.shape[0]
    hd = ML_HEAD_DIM
    heads = range(ML_HEADS)
    hsl = [slice(h * hd, (h + 1) * hd) for h in heads]
    row = lax.broadcasted_iota(jnp.int32, (cl, cl), 0)
    col = lax.broadcasted_iota(jnp.int32, (cl, cl), 1)
    tril = col <= row
    tril_bf = jnp.where(tril, 1.0, 0.0).astype(BF16)

    def wide(x):
        return jnp.concatenate([x, x], axis=1)

    log_f = jnp.minimum(gates, 0.0) - jnp.log(1.0 + jnp.exp(-jnp.abs(gates)))
    f_hi, f_lo = _split_bf16(log_f)
    b_all = _dot(tril_bf, f_hi) + _dot(tril_bf, f_lo)
    lane = lax.broadcasted_iota(jnp.int32, (cl, LANES), 1)
    rows_t = jnp.where(lane < ML_HEADS, gates, b_all).T
    ones = jnp.ones((cl, hd), BF16)

    qs = [q[:, hsl[h]] for h in heads]
    ks = [k[:, hsl[h]] for h in heads]
    v_aug = [jnp.concatenate([v[:, hsl[h]], ones], axis=1) for h in heads]
    b_rep = [jnp.broadcast_to(b_all[:, ML_HEADS + h:ML_HEADS + h + 1], (cl, LANES)) for h in heads]
    i_rep = [jnp.broadcast_to(gates[:, h:h + 1], (cl, LANES)) for h in heads]

    qk = [_dot_nt(qs[h], ks[h]) for h in heads]
    carried = [_dot(qs[h], c_prev[h].astype(BF16)) for h in heads]
    emit()

    d_mat, m_t, inter = [], [], []
    for h in heads:
        r_row = rows_t[h:h + 1, :] - rows_t[ML_HEADS + h:ML_HEADS + h + 1, :]
        d = jnp.where(tril, wide(b_rep[h]) + r_row, -jnp.inf)
        it = b_rep[h] + m_prev[h]
        d_mat.append(d)
        inter.append(it)
        m_t.append(jnp.maximum(it, jnp.max(d, axis=-1, keepdims=True)))
    emit()

    s_mat = [(jnp.exp(d_mat[h] - wide(m_t[h])) * qk[h]).astype(BF16) for h in heads]
    intra = [_dot(s_mat[h], v_aug[h]) for h in heads]
    emit()

    c_new, m_new = [], []
    for h in heads:
        b_last = b_rep[h][cl - 1:cl, :]
        g_rep = b_last - b_rep[h] + i_rep[h]
        m_n = jnp.maximum(b_last + m_prev[h], jnp.max(g_rep, axis=0, keepdims=True))
        decay = jnp.exp(b_last + m_prev[h] - m_n)
        w_k = jnp.exp(g_rep - m_n)
        wv = (wide(w_k) * v_aug[h].astype(F32)).astype(BF16)
        c_new.append(wide(decay) * c_prev[h] + _dot_tn(ks[h], wv))
        m_new.append(m_n)
    emit()

    outs = []
    for h in heads:
        w_inter = jnp.exp(inter[h] - m_t[h])
        num = w_inter * carried[h][:, :hd] + intra[h][:, :hd]
        den = w_inter * carried[h][:, hd:] + intra[h][:, hd:]
        hid = num / jnp.maximum(jnp.abs(den), jnp.exp(-m_t[h]))
        mu = jnp.mean(hid, axis=-1, keepdims=True)
        cen = hid - mu
        var = jnp.mean(cen * cen, axis=-1, keepdims=True)
        normed = cen * lax.rsqrt(var + LN_EPS) * norm_g[:, hsl[h]]
        o_h = og[:, hsl[h]].astype(F32)
        z_h = zg[:, hsl[h]].astype(F32)
        outs.append((normed * _sigmoid(o_h) * _silu(z_h)).astype(BF16))
    return outs, c_new, m_new


def _inproj_kernel(x_ref, xh_ref, mod_ref, w_ref, b_ref, wif_ref, bif_ref, cw_ref, cb_ref, ng_ref,
                   main_ref, uml_ref, c_scr, m_scr, *, tiles_per_seq):
    first = (pl.program_id(0) % tiles_per_seq) == 0

    @pl.when(first)
    def _():
        c_scr[...] = jnp.zeros_like(c_scr)
        m_scr[...] = jnp.zeros_like(m_scr)

    tm = x_ref.shape[0]
    shift = mod_ref[0, 0]
    scale = mod_ref[0, 1]
    h = (x_ref[...] * (1.0 + scale) + shift).astype(BF16)
    h_halo = (xh_ref[...] * (1.0 + scale) + shift).astype(BF16)
    h_ext = jnp.concatenate([h_halo, h], axis=0)

    def column_group(c0):
        cs = slice(c0, c0 + COL_GROUP)
        if ML_Q0 <= c0 < ML_Q0 + 2 * ML_WIDTH:
            ks = slice(c0 - ML_Q0, c0 - ML_Q0 + COL_GROUP)
            ext = _dot(h_ext, w_ref[:, cs]) + b_ref[:, cs]
            acc = ext[HALO:, :]
            halo = jnp.where(first, 0.0, ext[HALO - SUBLANES:HALO, :])
            cat = jnp.concatenate([halo, acc], axis=0)
            y = cb_ref[:, ks] + cw_ref[CONV_WIDTH - 1:CONV_WIDTH, ks] * acc
            for j in range(CONV_WIDTH - 1):
                shifted = pltpu.roll(cat, CONV_WIDTH - 1 - j, axis=0)[SUBLANES:, :]
                y = y + cw_ref[j:j + 1, ks] * shifted
            acc = _silu(y)
            if c0 < ML_Q0 + ML_WIDTH:
                acc = acc * (ML_HEAD_DIM ** -0.5)
            return acc
        acc = _dot(h, w_ref[:, cs]) + b_ref[:, cs]
        if SB0 <= c0 < SB0 + SB_WIDTH:
            acc = acc * (SB_HEAD_DIM ** -0.5)
        return acc

    ml = [column_group(ML_Q0 + g * ML_WIDTH).astype(BF16) for g in range(5)]
    gates = _dot(h, wif_ref[...]) + bif_ref[...]

    pending = list(range(0, main_ref.shape[1], COL_GROUP))
    n_chunks = tm // ML_CHUNK
    per_emit = -(-len(pending) // (MLSTM_EMIT_POINTS * n_chunks))

    def emit():
        for _ in range(min(per_emit, len(pending))):
            c0 = pending.pop(0)
            main_ref[:, c0:c0 + COL_GROUP] = column_group(c0).astype(BF16)

    c_state = [c_scr[hh] for hh in range(ML_HEADS)]
    m_state = [m_scr[hh] for hh in range(ML_HEADS)]
    norm_g = ng_ref[...]
    for n in range(n_chunks):
        rs = slice(n * ML_CHUNK, (n + 1) * ML_CHUNK)
        outs, c_state, m_state = _mlstm_chunk(ml[0][rs], ml[1][rs], ml[2][rs], ml[3][rs], ml[4][rs],
                                              gates[rs], norm_g, c_state, m_state, emit)
        for hh in range(ML_HEADS):
            uml_ref[rs, hh * ML_HEAD_DIM:(hh + 1) * ML_HEAD_DIM] = outs[hh]
    while pending:
        emit()
    for hh in range(ML_HEADS):
        c_scr[hh] = c_state[hh]
        m_scr[hh] = m_state[hh]


def _inproj(x2, layer, mod, wm, bm, wif, bif, cw, cb, norm_g, *, seq):
    t, d = x2.shape
    tm = INPROJ_TILE
    n_cols = wm.shape[2]
    tiles_per_seq = seq // tm
    const = lambda i: (layer, 0, 0)
    return pl.pallas_call(
        functools.partial(_inproj_kernel, tiles_per_seq=tiles_per_seq),
        grid=(t // tm,),
        in_specs=[
            pl.BlockSpec((tm, d), lambda i: (i, 0)),
            pl.BlockSpec((HALO, d), lambda i: (jnp.maximum(i * (tm // HALO) - 1, 0), 0)),
            pl.BlockSpec((None, 1, 3, 1, d), lambda i: (layer, i // tiles_per_seq, 0, 0, 0)),
            pl.BlockSpec((None, d, n_cols), const),
            pl.BlockSpec((None, 1, n_cols), const),
            pl.BlockSpec((None, d, LANES), const),
            pl.BlockSpec((None, 1, LANES), const),
            pl.BlockSpec((None, CONV_WIDTH, 2 * ML_WIDTH), const),
            pl.BlockSpec((None, 1, 2 * ML_WIDTH), const),
            pl.BlockSpec((None, 1, ML_WIDTH), const),
        ],
        out_specs=[
            pl.BlockSpec((tm, ML_Q0), lambda i: (i, 0)),
            pl.BlockSpec((tm, ML_WIDTH), lambda i: (i, 0)),
        ],
        out_shape=[
            jax.ShapeDtypeStruct((t, ML_Q0), BF16),
            jax.ShapeDtypeStruct((t, ML_WIDTH), BF16),
        ],
        scratch_shapes=[pltpu.VMEM((ML_HEADS, ML_HEAD_DIM, 2 * ML_HEAD_DIM), F32),
                        pltpu.VMEM((ML_HEADS, 1, LANES), F32)],
        compiler_params=pltpu.CompilerParams(
            dimension_semantics=("arbitrary",), vmem_limit_bytes=VMEM_LIMIT_BYTES),
        name="inproj_mlstm",
    )(x2, x2, mod, wm, bm, wif, bif, cw, cb, norm_g)


def _sb_kernel(q_ref, k_ref, v_ref, z_ref, o_ref, *, seq):
    ts = SB_TILE
    n_pairs = q_ref.shape[1] // LANES
    row = lax.broadcasted_iota(jnp.int32, (2 * ts, ts), 0)
    col = lax.broadcasted_iota(jnp.int32, (2 * ts, ts), 1)
    strict = col < (row & (ts - 1))
    upper = jnp.where(lax.broadcasted_iota(jnp.int32, (ts, ts), 0) > lax.broadcasted_iota(jnp.int32, (ts, ts), 1),
                      1.0, 0.0).astype(BF16)
    upper2 = jnp.concatenate([upper, upper], axis=0)
    head0 = lax.broadcasted_iota(jnp.int32, (ts, LANES), 1) < SB_HEAD_DIM

    psl = [slice(p * LANES, (p + 1) * LANES) for p in range(n_pairs)]

    def load_queries(q0):
        qs = []
        for p in range(n_pairs):
            q = q_ref[pl.ds(q0, ts), psl[p]]
            zero = jnp.zeros_like(q)
            qs.append(jnp.concatenate([jnp.where(head0, q, zero), jnp.where(head0, zero, q)], axis=0))
        return qs

    def key_tiles(entries, qs, carries, accs):
        group = SB_CUMSUM_PAIRS
        units = [(n, g) for n in range(len(entries)) for g in range(n_pairs // group)]
        log_betas, first_cols, suffixes = {}, {}, {}
        carries = [list(c) for c in carries]
        accs = [list(a) for a in accs]

        def stage1(n, g):
            a, j0, diag = entries[n]
            log_rems = []
            for p in range(g * group, (g + 1) * group):
                z = _dot_nt(qs[a][p], k_ref[pl.ds(j0, ts), psl[p]])
                log_beta = jnp.minimum(z, 0.0) - jnp.log(1.0 + jnp.exp2(jnp.abs(z) * -LOG2E))
                log_rem = log_beta - z
                if diag:
                    log_rem = jnp.where(strict, log_rem, 0.0)
                log_betas[n, p] = log_beta
                first_cols[n, p] = log_rem[:, 0:1]
                log_rems.append(log_rem)
            hi, lo = _split_bf16(jnp.concatenate(log_rems, axis=0))
            suffixes[n, g] = _dot(jnp.concatenate([hi, lo], axis=1), upper2)

        def stage2(n, g):
            a, j0, diag = entries[n]
            for r, p in enumerate(range(g * group, (g + 1) * group)):
                suffix = suffixes[n, g][2 * ts * r:2 * ts * (r + 1)]
                attn = jnp.exp((log_betas[n, p] + suffix) + carries[a][p])
                if diag:
                    attn = jnp.where(strict, attn, 0.0)
                accs[a][p] = accs[a][p] + _dot(attn.astype(BF16), v_ref[pl.ds(j0, ts), psl[p]])
                carries[a][p] = carries[a][p] + (suffix[:, 0:1] + first_cols[n, p])

        for step in range(len(units) + SB_STAGE_LAG):
            if step < len(units):
                stage1(*units[step])
            if step >= SB_STAGE_LAG:
                stage2(*units[step - SB_STAGE_LAG])
        return [tuple(c) for c in carries], [tuple(a) for a in accs]

    def alive(carries):
        top = carries[0]
        for c in carries[1:]:
            top = jnp.maximum(top, c)
        return jnp.max(top) > SB_LOG_CUTOFF

    def remaining_tiles(i, qs, carries, accs):
        def cond(st):
            jj, go, _, _ = st
            return jnp.logical_and(jj < i, go)

        def body(st):
            jj, _, carries, accs = st
            j0 = pl.multiple_of((i - 1 - jj) * ts, ts)
            (carries,), (accs,) = key_tiles([(0, j0, False)], [qs], [carries], [accs])
            return jj + 1, alive(carries), carries, accs

        _, _, carries, accs = lax.while_loop(cond, body, (jnp.int32(1), alive(carries), carries, accs))
        return carries, accs

    def q_tile_pair(m, first):
        ia = 2 * m
        if first:
            q0a, q0b = 0, ts
            entries = [(0, q0a, True), (1, q0b, True), (1, q0a, False)]
        else:
            q0a = pl.multiple_of(ia * ts, ts)
            q0b = pl.multiple_of(q0a + ts, ts)
            entries = [(0, q0a, True), (1, q0b, True), (0, pl.multiple_of(q0a - ts, ts), False), (1, q0a, False)]
        qs = [load_queries(q0a), load_queries(q0b)]
        zero_c = tuple(jnp.zeros((2 * ts, 1), F32) for _ in range(n_pairs))
        zero_a = tuple(jnp.zeros((2 * ts, LANES), F32) for _ in range(n_pairs))
        carries, accs = key_tiles(entries, qs, [zero_c, zero_c], [zero_a, zero_a])
        if not first:
            carries[0], accs[0] = remaining_tiles(ia, qs[0], carries[0], accs[0])
            carries[1], accs[1] = remaining_tiles(ia + 1, qs[1], carries[1], accs[1])
        for a, q0 in enumerate((q0a, q0b)):
            for p in range(n_pairs):
                acc = jnp.where(head0, accs[a][p][:ts], accs[a][p][ts:])
                gate = _silu(z_ref[pl.ds(q0, ts), psl[p]].astype(F32))
                o_ref[pl.ds(q0, ts), psl[p]] = (acc * gate).astype(BF16)

    q_tile_pair(0, True)

    def pair_body(m, carry):
        q_tile_pair(m, False)
        return carry

    lax.fori_loop(1, seq // (2 * ts), pair_body, 0)


def _sb_attention(main, *, batch, seq):
    t = main.shape[0]
    width = SB_PAIRS_PER_STEP * LANES
    n_steps = SB_WIDTH // width
    first = SB0 // width
    blk = lambda g: pl.BlockSpec((seq, width), lambda b, p: (b, first + g * n_steps + p))
    return pl.pallas_call(
        functools.partial(_sb_kernel, seq=seq),
        grid=(batch, n_steps),
        in_specs=[blk(0), blk(1), blk(2), blk(3)],
        out_specs=pl.BlockSpec((seq, width), lambda b, p: (b, p)),
        out_shape=jax.ShapeDtypeStruct((t, SB_WIDTH), BF16),
        compiler_params=pltpu.CompilerParams(
            dimension_semantics=("arbitrary", "arbitrary"), vmem_limit_bytes=VMEM_LIMIT_BYTES),
        name="sb_attention",
    )(main, main, main, main)


def _out_kernel(usb_ref, uml_ref, gate_ref, x_ref, mod_ref, wsb_ref, wml_ref, wo_ref,
                lng_ref, lnb_ref, o_ref, *, alpha):
    tm, d = x_ref.shape
    gate = mod_ref[0, 2]
    blocks = [slice(r0, r0 + OUT_ROW_BLOCK) for r0 in range(0, tm, OUT_ROW_BLOCK)]
    y_sb = [_dot(usb_ref[rs, :], wsb_ref[...]) for rs in blocks]
    y_ml = [_dot(uml_ref[rs, :], wml_ref[...]) for rs in blocks]
    mix = [(_sigmoid(gate_ref[rs, :d].astype(F32)) * y_sb[n]
            + _sigmoid(gate_ref[rs, d:].astype(F32)) * y_ml[n]).astype(BF16) for n, rs in enumerate(blocks)]
    y = [_dot(mix[n], wo_ref[...]) for n in range(len(blocks))]
    for n, rs in enumerate(blocks):
        r = alpha * x_ref[rs, :] + (1.0 + gate) * y[n]
        mu = jnp.mean(r, axis=-1, keepdims=True)
        cen = r - mu
        var = jnp.mean(cen * cen, axis=-1, keepdims=True)
        o_ref[rs, :] = cen * lax.rsqrt(var + LN_EPS) * lng_ref[...] + lnb_ref[...]


def _out_proj(usb, uml, main, x2, layer, mod, wsb, wml, wo, lng, lnb, *, seq, alpha):
    t, d = x2.shape
    tm = TOKEN_TILE
    tiles_per_seq = seq // tm
    const = lambda i: (layer, 0, 0)
    return pl.pallas_call(
        functools.partial(_out_kernel, alpha=alpha),
        grid=(t // tm,),
        in_specs=[
            pl.BlockSpec((tm, SB_WIDTH), lambda i: (i, 0)),
            pl.BlockSpec((tm, ML_WIDTH), lambda i: (i, 0)),
            pl.BlockSpec((tm, 2 * d), lambda i: (i, 0)),
            pl.BlockSpec((tm, d), lambda i: (i, 0)),
            pl.BlockSpec((None, 1, 3, 1, d), lambda i: (layer, i // tiles_per_seq, 0, 0, 0)),
            pl.BlockSpec((None, SB_WIDTH, d), const),
            pl.BlockSpec((None, ML_WIDTH, d), const),
            pl.BlockSpec((None, d, d), const),
            pl.BlockSpec((None, 1, d), const),
            pl.BlockSpec((None, 1, d), const),
        ],
        out_specs=pl.BlockSpec((tm, d), lambda i: (i, 0)),
        out_shape=jax.ShapeDtypeStruct((t, d), F32),
        compiler_params=pltpu.CompilerParams(
            dimension_semantics=("arbitrary",), vmem_limit_bytes=VMEM_LIMIT_BYTES),
        name="out_proj",
    )(usb, uml, main, x2, mod, wsb, wml, wo, lng, lnb)


def kernel(x, c, w_ada, b_ada, w_in, b_in, conv_w, conv_b, ml_norm_g, w_sb_proj, w_ml_proj, w_out, ln_g, ln_b):
    bsz, seq, d = x.shape
    depth = w_ada.shape[0]
    assert d == D_MODEL
    assert seq % TOKEN_TILE == 0 and seq % INPROJ_TILE == 0 and INPROJ_TILE % ML_CHUNK == 0
    assert seq % (2 * SB_TILE) == 0
    alpha = (2.0 * depth) ** 0.25
    if0 = 4 * SB_WIDTH + 5 * ML_WIDTH
    g_start = if0 + 2 * ML_HEADS

    mod_all = _adaln(c, w_ada, b_ada).reshape(depth, bsz, 3, 1, d)
    pad = LANES - 2 * ML_HEADS
    wm = jnp.concatenate([w_in[:, :, g_start:], w_in[:, :, :if0]], axis=2).astype(BF16)
    bm = jnp.concatenate([b_in[:, g_start:], b_in[:, :if0]], axis=1)[:, None, :]
    wif = jnp.pad(w_in[:, :, if0:g_start], ((0, 0), (0, 0), (0, pad))).astype(BF16)
    bif = jnp.pad(b_in[:, if0:g_start], ((0, 0), (0, pad)))[:, None, :]
    wsb, wml, wo = w_sb_proj.astype(BF16), w_ml_proj.astype(BF16), w_out.astype(BF16)
    cb, norm_g, lng, lnb = conv_b[:, None, :], ml_norm_g[:, None, :], ln_g[:, None, :], ln_b[:, None, :]

    x2 = x.reshape(bsz * seq, d)
    for l in range(depth):
        main, uml = _inproj(x2, l, mod_all, wm, bm, wif, bif, conv_w, cb, norm_g, seq=seq)
        usb = _sb_attention(main, batch=bsz, seq=seq)
        x2 = _out_proj(usb, uml, main, x2, l, mod_all, wsb, wml, wo, lng, lnb, seq=seq, alpha=alpha)
    return x2.reshape(bsz, seq, d)
```
